```python
import math
import jax, jax.numpy as jnp
from jax import lax
import numpy as np


D_MODEL = 1024
BATCH = 32
SEQ = 2048
DEPTH = 4

CHUNK = 64
MIX_WIDTH = D_MODEL
POOL_WIDTH = MIX_WIDTH // 2
POOL_WINDOWS = (2, 4, 8, 16)
POOL_GROUPS = len(POOL_WINDOWS)
POOL_GROUP = POOL_WIDTH // POOL_GROUPS
DN_HEAD_DIM = 128
DN_HEADS = (MIX_WIDTH - POOL_WIDTH) // DN_HEAD_DIM
DN_WIDTH = DN_HEADS * DN_HEAD_DIM
DN_CONV = 4
SGU_WIDTH = MIX_WIDTH // 2
SGU_BLOCK = 128
SGU_HEADS = 4
SGU_HEAD_CH = SGU_WIDTH // SGU_HEADS
SC_WIDTH = MIX_WIDTH - SGU_WIDTH
SC_CONV = 3
FFN_DIM = ((8 * D_MODEL // 3 + 127) // 128) * 128
AB_IN = POOL_WIDTH + 4 * DN_WIDTH + 2 * DN_HEADS
CD_IN = 2 * SGU_WIDTH + 3 * SC_WIDTH
N_EVEN = (DEPTH + 1) // 2
N_ODD = DEPTH // 2
EPS = 1e-6

kernel_name = 'hybrid_chunk_causal_encoder'


def rmsnorm(x, g):
    xf = x.astype(jnp.float32)
    y = xf * lax.rsqrt(jnp.mean(xf * xf, axis=-1, keepdims=True) + EPS)
    return (y * g.astype(jnp.float32)).astype(x.dtype)


def layernorm(x, g, b):
    xf = x.astype(jnp.float32)
    mu = jnp.mean(xf, axis=-1, keepdims=True)
    xc = xf - mu
    y = xc * lax.rsqrt(jnp.mean(xc * xc, axis=-1, keepdims=True) + EPS)
    return (y * g.astype(jnp.float32) + b.astype(jnp.float32)).astype(x.dtype)


def l2norm(x):
    return x * lax.rsqrt(jnp.sum(x * x, axis=-1, keepdims=True) + EPS)


def causal_depthwise_conv(x, w):
    K, C = w.shape
    return lax.conv_general_dilated(
        x, w[:, None, :].astype(x.dtype), window_strides=(1,), padding=[(K - 1, 0)],
        dimension_numbers=('NWC', 'WIO', 'NWC'), feature_group_count=C)


def swiglu(x, w_gate, w_up, w_down):
    return (jax.nn.silu(x @ w_gate) * (x @ w_up)) @ w_down


def pool_mixer(a, w_pool, scale):
    B, S, _ = a.shape
    grp = a.astype(jnp.float32).reshape(B, S, POOL_GROUPS, POOL_GROUP)
    cs = jnp.cumsum(grp, axis=1)
    pos = jnp.arange(S)
    outs = []
    for gi, w in enumerate(POOL_WINDOWS):
        c = cs[:, :, gi]
        lagged = jnp.pad(c, ((0, 0), (w, 0), (0, 0)))[:, :S]
        cnt = jnp.minimum(pos + 1, w).astype(jnp.float32)[None, :, None]
        outs.append((c - lagged) / cnt - grp[:, :, gi])
    pooled = jnp.stack(outs, axis=2).astype(a.dtype)
    mixed = jnp.einsum('bsgc,gcd->bsgd', pooled, w_pool)
    return mixed.reshape(B, S, POOL_WIDTH) * scale


def gated_delta_rule(q, k, v, beta, g):
    B, S, H, dk = q.shape
    dv = v.shape[-1]
    N = S // CHUNK
    def chunks(t):
        t = jnp.swapaxes(t, 1, 2)
        return t.reshape((B, H, N, CHUNK) + t.shape[3:])
    q, k, v, beta, g = chunks(q), chunks(k), chunks(v), chunks(beta), chunks(g)
    gc = jnp.cumsum(g, axis=-1)
    tril = jnp.tril(jnp.ones((CHUNK, CHUNK), dtype=bool))
    strict = jnp.tril(jnp.ones((CHUNK, CHUNK), dtype=bool), k=-1)
    gamma = jnp.exp(jnp.where(tril, gc[..., :, None] - gc[..., None, :], -jnp.inf))
    kb = k * beta[..., None]
    lmat = jnp.where(strict, jnp.einsum('bhnid,bhnjd->bhnij', kb, k) * gamma, 0.0)
    eye = jnp.eye(CHUNK, dtype=q.dtype)
    rhs = jnp.concatenate([v * beta[..., None], kb * jnp.exp(gc)[..., None]], axis=-1)
    sol = lax.linalg.triangular_solve(eye + lmat, rhs, left_side=True, lower=True,
                                      unit_diagonal=True)
    u, w = sol[..., :dv], sol[..., dv:]
    aqk = jnp.einsum('bhnid,bhnjd->bhnij', q, k) * gamma
    q_dec = q * jnp.exp(gc)[..., None]
    k_dec = k * jnp.exp(gc[..., -1:] - gc)[..., None]
    last = jnp.exp(gc[..., -1])

    def step(state, xs):
        u_i, w_i, qd_i, a_i, kd_i, l_i = xs
        v_new = u_i - jnp.einsum('bhcd,bhde->bhce', w_i, state)
        o_i = (jnp.einsum('bhcd,bhde->bhce', qd_i, state)
               + jnp.einsum('bhij,bhje->bhie', a_i, v_new))
        state = state * l_i[..., None, None] + jnp.einsum('bhcd,bhce->bhde', kd_i, v_new)
        return state, o_i

    xs = tuple(jnp.moveaxis(t, 2, 0) for t in (u, w, q_dec, aqk, k_dec, last))
    s0 = jnp.zeros((B, H, dk, dv), q.dtype)
    _, o = lax.scan(step, s0, xs)
    return o.transpose(1, 0, 3, 2, 4).reshape(B, S, H, dv)


def mixer_ab(h, w_in, pool_w, pool_scale, conv_w, a_log, dt_bias, out_norm, w_out):
    B, S, _ = h.shape
    proj = h @ w_in
    o0 = POOL_WIDTH
    o1 = o0 + 3 * DN_WIDTH
    o2 = o1 + DN_WIDTH
    o3 = o2 + DN_HEADS
    a_in, qkv, z = proj[..., :o0], proj[..., o0:o1], proj[..., o1:o2]
    b_raw, g_raw = proj[..., o2:o3], proj[..., o3:]
    y_a = pool_mixer(a_in, pool_w, pool_scale)
    qkv = jax.nn.silu(causal_depthwise_conv(qkv, conv_w)).astype(jnp.float32)
    q, k, v = (t.reshape(B, S, DN_HEADS, DN_HEAD_DIM) for t in jnp.split(qkv, 3, axis=-1))
    q = l2norm(q) * (DN_HEAD_DIM ** -0.5)
    k = l2norm(k)
    beta = jax.nn.sigmoid(b_raw.astype(jnp.float32))
    g = -jnp.exp(a_log.astype(jnp.float32)) * jax.nn.softplus(
        g_raw.astype(jnp.float32) + dt_bias.astype(jnp.float32))
    o = gated_delta_rule(q, k, v, beta, g)
    o = rmsnorm(o, out_norm) * jax.nn.silu(z.reshape(B, S, DN_HEADS, DN_HEAD_DIM).astype(jnp.float32))
    y_b = o.reshape(B, S, DN_WIDTH).astype(h.dtype)
    return jnp.concatenate([y_a, y_b], axis=-1) @ w_out


def mixer_cd(h, w_in, sgu_norm_g, sgu_norm_b, sgu_w, sgu_bias, conv_w, w_out):
    B, S, _ = h.shape
    proj = h @ w_in
    uv = jax.nn.gelu(proj[..., :2 * SGU_WIDTH])
    u, v = uv[..., :SGU_WIDTH], uv[..., SGU_WIDTH:]
    v = layernorm(v, sgu_norm_g, sgu_norm_b)
    nb = S // SGU_BLOCK
    vb = v.reshape(B, nb, SGU_BLOCK, SGU_HEADS, SGU_HEAD_CH)
    mask = jnp.tril(jnp.ones((SGU_BLOCK, SGU_BLOCK), dtype=bool))
    ws = jnp.where(mask, sgu_w, 0.0).astype(v.dtype)
    mixed = jnp.einsum('hij,bnjhc->bnihc', ws, vb) + sgu_bias.T[None, None, :, :, None]
    y_c = u * mixed.reshape(B, S, SGU_WIDTH)
    sc = proj[..., 2 * SGU_WIDTH:]
    xd, bg, cg = sc[..., :SC_WIDTH], sc[..., SC_WIDTH:2 * SC_WIDTH], sc[..., 2 * SC_WIDTH:]
    y_d = bg * causal_depthwise_conv(cg * xd, conv_w)
    return jnp.concatenate([y_c, y_d], axis=-1) @ w_out


def setup_inputs(seed: int = 0) -> dict:
    key = jax.random.key(seed)
    ks = jax.random.split(key, 32)
    f32 = jnp.float32
    def nrm(k, shape, scale):
        return jax.random.normal(k, shape, f32) * scale
    def gain(k, shape):
        return 1.0 + 0.02 * jax.random.normal(k, shape, f32)
    D, F = D_MODEL, FFN_DIM
    dt = jnp.exp(jax.random.uniform(ks[14], (N_EVEN, DN_HEADS), f32,
                                    minval=math.log(1e-3), maxval=math.log(1e-1)))
    return {
        'x': jax.random.normal(ks[0], (BATCH, SEQ, D), f32),
        'ffn1_norm': gain(ks[1], (DEPTH, D)),
        'ffn1_w_gate': nrm(ks[2], (DEPTH, D, F), D ** -0.5),
        'ffn1_w_up': nrm(ks[3], (DEPTH, D, F), D ** -0.5),
        'ffn1_w_down': nrm(ks[4], (DEPTH, F, D), F ** -0.5),
        'mix_norm': gain(ks[5], (DEPTH, D)),
        'ffn2_norm': gain(ks[6], (DEPTH, D)),
        'ffn2_w_gate': nrm(ks[7], (DEPTH, D, F), D ** -0.5),
        'ffn2_w_up': nrm(ks[8], (DEPTH, D, F), D ** -0.5),
        'ffn2_w_down': nrm(ks[9], (DEPTH, F, D), F ** -0.5),
        'ab_w_in': nrm(ks[10], (N_EVEN, D, AB_IN), D ** -0.5),
        'pool_w': nrm(ks[11], (N_EVEN, POOL_GROUPS, POOL_GROUP, POOL_GROUP), POOL_GROUP ** -0.5),
        'pool_scale': gain(ks[12], (N_EVEN, POOL_WIDTH)),
        'dn_conv_w': nrm(ks[13], (N_EVEN, DN_CONV, 3 * DN_WIDTH), DN_CONV ** -0.5),
        'dn_a_log': jnp.log(jax.random.uniform(ks[15], (N_EVEN, DN_HEADS), f32, minval=1.0, maxval=16.0)),
        'dn_dt_bias': dt + jnp.log(-jnp.expm1(-dt)),
        'dn_out_norm': gain(ks[16], (N_EVEN, DN_HEAD_DIM)),
        'ab_w_out': nrm(ks[17], (N_EVEN, MIX_WIDTH, D), MIX_WIDTH ** -0.5),
        'cd_w_in': nrm(ks[18], (N_ODD, D, CD_IN), D ** -0.5),
        'sgu_norm_g': gain(ks[19], (N_ODD, SGU_WIDTH)),
        'sgu_norm_b': nrm(ks[20], (N_ODD, SGU_WIDTH), 0.02),
        'sgu_w': nrm(ks[21], (N_ODD, SGU_HEADS, SGU_BLOCK, SGU_BLOCK), SGU_BLOCK ** -0.5),
        'sgu_bias': gain(ks[22], (N_ODD, SGU_HEADS, SGU_BLOCK)),
        'sc_conv_w': nrm(ks[23], (N_ODD, SC_CONV, SC_WIDTH), SC_CONV ** -0.5),
        'cd_w_out': nrm(ks[24], (N_ODD, MIX_WIDTH, D), MIX_WIDTH ** -0.5),
        'final_norm': gain(ks[25], (D,)),
    }


def reference(x, ffn1_norm, ffn1_w_gate, ffn1_w_up, ffn1_w_down, mix_norm,
              ffn2_norm, ffn2_w_gate, ffn2_w_up, ffn2_w_down,
              ab_w_in, pool_w, pool_scale, dn_conv_w, dn_a_log, dn_dt_bias, dn_out_norm, ab_w_out,
              cd_w_in, sgu_norm_g, sgu_norm_b, sgu_w, sgu_bias, sc_conv_w, cd_w_out,
              final_norm):
    h = x
    for layer in range(DEPTH):
        h = h + 0.5 * swiglu(rmsnorm(h, ffn1_norm[layer]), ffn1_w_gate[layer],
                             ffn1_w_up[layer], ffn1_w_down[layer])
        hn = rmsnorm(h, mix_norm[layer])
        if layer % 2 == 0:
            e = layer // 2
            h = h + mixer_ab(hn, ab_w_in[e], pool_w[e], pool_scale[e], dn_conv_w[e],
                             dn_a_log[e], dn_dt_bias[e], dn_out_norm[e], ab_w_out[e])
        else:
            o = layer // 2
            h = h + mixer_cd(hn, cd_w_in[o], sgu_norm_g[o], sgu_norm_b[o], sgu_w[o],
                             sgu_bias[o], sc_conv_w[o], cd_w_out[o])
        h = h + 0.5 * swiglu(rmsnorm(h, ffn2_norm[layer]), ffn2_w_gate[layer],
                             ffn2_w_up[layer], ffn2_w_down[layer])
    return rmsnorm(h, final_norm)
```

```python
import functools

import jax
import jax.numpy as jnp
from jax import lax
from jax.experimental import pallas as pl
from jax.experimental.pallas import tpu as pltpu

F32 = jnp.float32
BF16 = jnp.bfloat16
EPS = 1e-6

LANES = 128
CHUNK = 64
POOL_WINDOWS = (2, 4, 8, 16)
POOL_HIST = 16
CONV_HIST = 8
DN_CONV = 4
SC_CONV = 3
SGU_BLOCK = 128
FFN_CHUNK = 256
NEUMANN_STEPS = 5

FFN_ROWS = 512
MIX_ROWS = 256
VMEM_LIMIT = 56 * 1024 * 1024


def _dot(a, b):
    return jnp.dot(a, b, preferred_element_type=F32)


def _dot_nt(a, b):
    return lax.dot_general(a, b, (((1,), (1,)), ((), ())), preferred_element_type=F32)


def _dot_tn(a, b):
    return lax.dot_general(a, b, (((0,), (0,)), ((), ())), preferred_element_type=F32)


def _split3(x):
    hi = x.astype(BF16)
    r = x - hi.astype(F32)
    mid = r.astype(BF16)
    lo = (r - mid.astype(F32)).astype(BF16)
    return hi, mid, lo


def _dot_exact_rhs(a01, x):
    hi, mid, lo = _split3(x)
    return _dot(a01, hi) + _dot(a01, mid) + _dot(a01, lo)


def _dot_exact_lhs(x, b01):
    hi, mid, lo = _split3(x)
    return _dot(hi, b01) + _dot(mid, b01) + _dot(lo, b01)


def _rmsnorm(x, g):
    return x * lax.rsqrt(jnp.mean(x * x, axis=-1, keepdims=True) + EPS) * g


def _sigmoid(x):
    return 1.0 / (1.0 + jnp.exp(-x))


def _silu(x):
    return x * _sigmoid(x)


def _gelu_tanh(x):
    return 0.5 * x * (1.0 + jnp.tanh(0.7978845608028654 * (x + 0.044715 * (x * x * x))))


def _softplus(x):
    return jnp.maximum(x, 0.0) + jnp.log1p(jnp.exp(-jnp.abs(x)))


def _ffn_body(*refs, n_chunks, final):
    if final:
        h_ref, g_ref, wg_ref, wu_ref, wd_ref, fg_ref, o_ref = refs
    else:
        h_ref, g_ref, wg_ref, wu_ref, wd_ref, o_ref = refs
    h = h_ref[...]
    hn = _rmsnorm(h, g_ref[...]).astype(BF16)
    acc = None
    for c in range(n_chunks):
        cols = slice(c * FFN_CHUNK, (c + 1) * FFN_CHUNK)
        gate = _dot(hn, wg_ref[:, cols])
        up = _dot(hn, wu_ref[:, cols])
        act = (_silu(gate) * up).astype(BF16)
        part = _dot(act, wd_ref[cols, :])
        acc = part if acc is None else acc + part
    out = h + 0.5 * acc
    if final:
        out = _rmsnorm(out, fg_ref[...])
    o_ref[...] = out


def _const_spec(shape):
    zeros = (0,) * len(shape)
    return pl.BlockSpec(shape, lambda *_: zeros, pipeline_mode=pl.Buffered(1))


def _ffn(h, norm_g, w_gate, w_up, w_down, final_g=None):
    n, d = h.shape
    f = w_gate.shape[1]
    n_chunks = f // FFN_CHUNK
    rows = min(FFN_ROWS, n)
    wg = w_gate.astype(BF16)
    wu = w_up.astype(BF16)
    wd = w_down.astype(BF16)
    row_spec = pl.BlockSpec((rows, d), lambda i: (i, 0))
    args = [h, norm_g.reshape(1, d), wg, wu, wd]
    in_specs = [row_spec, _const_spec((1, d)), _const_spec(wg.shape), _const_spec(wu.shape),
                _const_spec(wd.shape)]
    if final_g is not None:
        args.append(final_g.reshape(1, d))
        in_specs.append(_const_spec((1, d)))
    return pl.pallas_call(
        functools.partial(_ffn_body, n_chunks=n_chunks, final=final_g is not None),
        grid=(n // rows,),
        in_specs=in_specs,
        out_specs=row_spec,
        out_shape=jax.ShapeDtypeStruct((n, d), F32),
        compiler_params=pltpu.CompilerParams(
            dimension_semantics=("parallel",), vmem_limit_bytes=VMEM_LIMIT),
        name="ffn_final" if final_g is not None else "ffn",
    )(*args)


def _mixer_ab_body(h_ref, g_ref, w_main_ref, w_bg_ref, pool_w_ref, pool_scale_ref, conv_w_ref,
                   a_log_ref, dt_bias_ref, out_norm_ref, expand_ref, w_out_ref, o_ref,
                   a_ext, qkv_ext, state, *, ts, pool_width, dn_width, heads):
    t = pl.program_id(1)
    hd = LANES
    n_chunks = ts // CHUNK

    @pl.when(t == 0)
    def _():
        a_ext[0:POOL_HIST, :] = jnp.zeros((POOL_HIST, pool_width), F32)
        qkv_ext[0:CONV_HIST, :] = jnp.zeros((CONV_HIST, 3 * dn_width), F32)
        state[...] = jnp.zeros(state.shape, F32)

    h = h_ref[0]
    hn = _rmsnorm(h, g_ref[...]).astype(BF16)
    proj = _dot(hn, w_main_ref[...])
    bgp = _dot(hn, w_bg_ref[...])

    a_in = proj[:, :pool_width]
    a_ext[POOL_HIST:POOL_HIST + ts, :] = a_in
    pos = t * ts + lax.broadcasted_iota(jnp.int32, (ts, LANES), 0)
    y_parts = []
    for gi, win in enumerate(POOL_WINDOWS):
        cols = slice(gi * LANES, (gi + 1) * LANES)
        cur = a_in[:, cols]
        s = cur
        for j in range(1, win):
            s = s + a_ext[POOL_HIST - j:POOL_HIST - j + ts, cols]
        cnt = jnp.minimum(pos + 1, win).astype(F32)
        pooled = s / cnt - cur
        mixed = _dot(pooled.astype(BF16), pool_w_ref[gi])
        y_parts.append(mixed * pool_scale_ref[:, cols])
    a_ext[0:POOL_HIST, :] = a_ext[ts:ts + POOL_HIST, :]

    o0 = pool_width
    qkv_ext[CONV_HIST:CONV_HIST + ts, :] = proj[:, o0:o0 + 3 * dn_width]
    conv = conv_w_ref[DN_CONV - 1:DN_CONV, :] * qkv_ext[CONV_HIST:CONV_HIST + ts, :]
    for j in range(DN_CONV - 1):
        lo = CONV_HIST - (DN_CONV - 1) + j
        conv = conv + conv_w_ref[j:j + 1, :] * qkv_ext[lo:lo + ts, :]
    qkv_ext[0:CONV_HIST, :] = qkv_ext[ts:ts + CONV_HIST, :]
    qkv = _silu(conv)
    z = proj[:, o0 + 3 * dn_width:o0 + 4 * dn_width]

    lane = lax.broadcasted_iota(jnp.int32, (ts, LANES), 1)
    beta_all = _sigmoid(bgp)
    g_all = -jnp.exp(a_log_ref[...]) * _softplus(bgp + dt_bias_ref[...])
    gates = _dot_exact_lhs(jnp.where(lane < heads, beta_all, g_all), expand_ref[...])
    beta_x = gates[:, :dn_width]
    g_x = gates[:, dn_width:]

    ri = lax.broadcasted_iota(jnp.int32, (ts, ts), 0)
    ci = lax.broadcasted_iota(jnp.int32, (ts, ts), 1)
    same = (ri ^ ci) < CHUNK
    tril = same & (ci <= ri)
    strict = same & (ci < ri)
    tril01 = jnp.where(tril, 1.0, 0.0).astype(BF16)
    strict01 = jnp.where(strict, 1.0, 0.0)
    eye = jnp.where(ri == ci, 1.0, 0.0)
    gc_x = _dot_exact_rhs(tril01, g_x)

    o_heads = []
    for hh in range(heads):
        hs = slice(hh * hd, (hh + 1) * hd)
        q = qkv[:, hs]
        k = qkv[:, dn_width + hh * hd:dn_width + (hh + 1) * hd]
        v = qkv[:, 2 * dn_width + hh * hd:2 * dn_width + (hh + 1) * hd]
        q = q * lax.rsqrt(jnp.sum(q * q, axis=-1, keepdims=True) + EPS) * (hd ** -0.5)
        k = k * lax.rsqrt(jnp.sum(k * k, axis=-1, keepdims=True) + EPS)
        beta = beta_x[:, hs]
        g = g_x[:, hs]
        gc = gc_x[:, hs]
        eg = jnp.exp(gc)
        g_wide = jnp.concatenate([g] * (ts // hd), axis=1)
        dm = _dot_exact_rhs(tril01, g_wide * strict01)
        gamma = jnp.where(tril, jnp.exp(dm), 0.0)
        kb = k * beta
        k16 = k.astype(BF16)
        lmat = jnp.where(strict, _dot_nt(kb.astype(BF16), k16) * gamma, 0.0)
        l16 = lmat.astype(BF16)
        inv = eye - lmat
        pw = _dot(l16, l16)
        for step in range(NEUMANN_STEPS):
            p16 = pw.astype(BF16)
            inv = inv + _dot(inv.astype(BF16), p16)
            if step + 1 < NEUMANN_STEPS:
                pw = _dot(p16, p16)
        rhs = jnp.concatenate([v * beta, kb * eg], axis=1).astype(BF16)
        sol = _dot(inv.astype(BF16), rhs)
        u = sol[:, :hd]
        w = sol[:, hd:]
        aqk = (_dot_nt(q.astype(BF16), k16) * gamma).astype(BF16)
        qd = q * eg

        s_h = state[hh]
        vn_parts = []
        o_parts = []
        for c in range(n_chunks):
            rows = slice(c * CHUNK, (c + 1) * CHUNK)
            wq = jnp.concatenate([w[rows], qd[rows]], axis=0).astype(BF16)
            r = _dot(wq, s_h.astype(BF16))
            v_new = u[rows] - r[:CHUNK]
            vn_parts.append(v_new.astype(BF16))
            pad = [jnp.zeros((CHUNK, hd), BF16)] * (n_chunks - c - 1)
            vn_buf = jnp.concatenate(vn_parts + pad, axis=0)
            o_parts.append(r[CHUNK:] + _dot(aqk[rows, :], vn_buf))
            g_last = gc[(c + 1) * CHUNK - 1:(c + 1) * CHUNK, :]
            k_dec = k[rows] * jnp.exp(g_last - gc[rows])
            s_h = s_h * jnp.exp(g_last) + _dot_tn(k_dec.astype(BF16), v_new.astype(BF16))
        state[hh] = s_h
        o = jnp.concatenate(o_parts, axis=0)
        o = _rmsnorm(o, out_norm_ref[...]) * _silu(z[:, hs])
        o_heads.append(o)

    y = jnp.concatenate(y_parts + o_heads, axis=1).astype(BF16)
    o_ref[0] = h + _dot(y, w_out_ref[...])


def _mixer_ab(h, norm_g, w_in, pool_w, pool_scale, conv_w, a_log, dt_bias, out_norm, w_out):
    b, s, d = h.shape
    heads = a_log.shape[0]
    pool_width = pool_scale.shape[0]
    dn_width = heads * LANES
    main = pool_width + 4 * dn_width
    ts = min(MIX_ROWS, s)
    w_main = w_in[:, :main].astype(BF16)
    w_bg = jnp.pad(w_in[:, main:main + 2 * heads], ((0, 0), (0, LANES - 2 * heads))).astype(BF16)
    gate_row = lambda p: jnp.pad(p, (heads, LANES - 2 * heads)).reshape(1, LANES)
    r_idx = jnp.arange(LANES)[:, None]
    c_idx = jnp.arange(2 * dn_width)[None, :]
    expand = (r_idx == c_idx // LANES).astype(BF16)
    tok_spec = pl.BlockSpec((1, ts, d), lambda i, j: (i, j, 0))
    args = [h, norm_g.reshape(1, d), w_main, w_bg, pool_w.astype(BF16),
            pool_scale.reshape(1, pool_width), conv_w, gate_row(a_log), gate_row(dt_bias),
            out_norm.reshape(1, LANES), expand, w_out.astype(BF16)]
    in_specs = [tok_spec] + [_const_spec(a.shape) for a in args[1:]]
    return pl.pallas_call(
        functools.partial(_mixer_ab_body, ts=ts, pool_width=pool_width, dn_width=dn_width,
                          heads=heads),
        grid=(b, s // ts),
        in_specs=in_specs,
        out_specs=tok_spec,
        out_shape=jax.ShapeDtypeStruct((b, s, d), F32),
        scratch_shapes=[pltpu.VMEM((POOL_HIST + ts, pool_width), F32),
                        pltpu.VMEM((CONV_HIST + ts, 3 * dn_width), F32),
                        pltpu.VMEM((heads, LANES, LANES), F32)],
        compiler_params=pltpu.CompilerParams(
            dimension_semantics=("arbitrary", "arbitrary"), vmem_limit_bytes=VMEM_LIMIT),
        name="mixer_ab",
    )(*args)


def _mixer_cd_body(h_ref, g_ref, w_in_ref, ln_g_ref, ln_b_ref, sgu_w_ref, sgu_bias_ref,
                   conv_w_ref, w_out_ref, o_ref, p_ext, *, ts, sgu_width, sc_width, heads):
    t = pl.program_id(1)
    n_blocks = ts // SGU_BLOCK

    @pl.when(t == 0)
    def _():
        p_ext[0:CONV_HIST, :] = jnp.zeros((CONV_HIST, sc_width), F32)

    h = h_ref[0]
    hn = _rmsnorm(h, g_ref[...]).astype(BF16)
    proj = _dot(hn, w_in_ref[...])
    uv = _gelu_tanh(proj[:, :2 * sgu_width])
    u = uv[:, :sgu_width]
    v = uv[:, sgu_width:]
    mu = jnp.mean(v, axis=-1, keepdims=True)
    vc = v - mu
    v = vc * lax.rsqrt(jnp.mean(vc * vc, axis=-1, keepdims=True) + EPS) * ln_g_ref[...] + ln_b_ref[...]
    v16 = v.astype(BF16)

    ri = lax.broadcasted_iota(jnp.int32, (SGU_BLOCK, SGU_BLOCK), 0)
    ci = lax.broadcasted_iota(jnp.int32, (SGU_BLOCK, SGU_BLOCK), 1)
    head_cols = []
    for hh in range(heads):
        hs = slice(hh * LANES, (hh + 1) * LANES)
        ws = jnp.where(ci <= ri, sgu_w_ref[hh], 0.0).astype(BF16)
        rhs = jnp.concatenate([v16[n * SGU_BLOCK:(n + 1) * SGU_BLOCK, hs] for n in range(n_blocks)],
                              axis=1)
        m = _dot(ws, rhs)
        head_cols.append(jnp.concatenate(
            [m[:, n * LANES:(n + 1) * LANES] for n in range(n_blocks)], axis=0))
    mixed = jnp.concatenate(head_cols, axis=1)
    bias = jnp.concatenate([sgu_bias_ref[...]] * n_blocks, axis=0)
    y_c = u * (mixed + bias)

    o0 = 2 * sgu_width
    xd = proj[:, o0:o0 + sc_width]
    bg = proj[:, o0 + sc_width:o0 + 2 * sc_width]
    cg = proj[:, o0 + 2 * sc_width:o0 + 3 * sc_width]
    p_ext[CONV_HIST:CONV_HIST + ts, :] = cg * xd
    conv = conv_w_ref[SC_CONV - 1:SC_CONV, :] * p_ext[CONV_HIST:CONV_HIST + ts, :]
    for j in range(SC_CONV - 1):
        lo = CONV_HIST - (SC_CONV - 1) + j
        conv = conv + conv_w_ref[j:j + 1, :] * p_ext[lo:lo + ts, :]
    p_ext[0:CONV_HIST, :] = p_ext[ts:ts + CONV_HIST, :]
    y_d = bg * conv

    y = jnp.concatenate([y_c, y_d], axis=1).astype(BF16)
    o_ref[0] = h + _dot(y, w_out_ref[...])


def _mixer_cd(h, norm_g, w_in, ln_g, ln_b, sgu_w, sgu_bias, conv_w, w_out):
    b, s, d = h.shape
    heads = sgu_w.shape[0]
    sgu_width = ln_g.shape[0]
    sc_width = conv_w.shape[1]
    ts = min(MIX_ROWS, s)
    bias_x = jnp.repeat(sgu_bias.T, LANES, axis=1)
    tok_spec = pl.BlockSpec((1, ts, d), lambda i, j: (i, j, 0))
    args = [h, norm_g.reshape(1, d), w_in.astype(BF16), ln_g.reshape(1, sgu_width),
            ln_b.reshape(1, sgu_width), sgu_w, bias_x, conv_w, w_out.astype(BF16)]
    in_specs = [tok_spec] + [_const_spec(a.shape) for a in args[1:]]
    return pl.pallas_call(
        functools.partial(_mixer_cd_body, ts=ts, sgu_width=sgu_width, sc_width=sc_width,
                          heads=heads),
        grid=(b, s // ts),
        in_specs=in_specs,
        out_specs=tok_spec,
        out_shape=jax.ShapeDtypeStruct((b, s, d), F32),
        scratch_shapes=[pltpu.VMEM((CONV_HIST + ts, sc_width), F32)],
        compiler_params=pltpu.CompilerParams(
            dimension_semantics=("arbitrary", "arbitrary"), vmem_limit_bytes=VMEM_LIMIT),
        name="mixer_cd",
    )(*args)


def kernel(x, ffn1_norm, ffn1_w_gate, ffn1_w_up, ffn1_w_down, mix_norm, ffn2_norm, ffn2_w_gate, ffn2_w_up, ffn2_w_down, ab_w_in, pool_w, pool_scale, dn_conv_w, dn_a_log, dn_dt_bias, dn_out_norm, ab_w_out, cd_w_in, sgu_norm_g, sgu_norm_b, sgu_w, sgu_bias, sc_conv_w, cd_w_out, final_norm):
    b, s, d = x.shape
    depth = ffn1_norm.shape[0]
    h = x.reshape(b * s, d)
    for layer in range(depth):
        h = _ffn(h, ffn1_norm[layer], ffn1_w_gate[layer], ffn1_w_up[layer], ffn1_w_down[layer])
        h3 = h.reshape(b, s, d)
        if layer % 2 == 0:
            e = layer // 2
            h3 = _mixer_ab(h3, mix_norm[layer], ab_w_in[e], pool_w[e], pool_scale[e], dn_conv_w[e],
                           dn_a_log[e], dn_dt_bias[e], dn_out_norm[e], ab_w_out[e])
        else:
            o = layer // 2
            h3 = _mixer_cd(h3, mix_norm[layer], cd_w_in[o], sgu_norm_g[o], sgu_norm_b[o], sgu_w[o],
                           sgu_bias[o], sc_conv_w[o], cd_w_out[o])
        h = h3.reshape(b * s, d)
        h = _ffn(h, ffn2_norm[layer], ffn2_w_gate[layer], ffn2_w_up[layer], ffn2_w_down[layer],
                 final_g=final_norm if layer == depth - 1 else None)
    return h.reshape(b, s, d)
```

```python
import functools

import jax
import jax.numpy as jnp
from jax import lax
from jax.experimental import pallas as pl
from jax.experimental.pallas import tpu as pltpu

F32 = jnp.float32
BF16 = jnp.bfloat16
EPS = 1e-6

LANES = 128
CHUNK = 64
POOL_WINDOWS = (2, 4, 8, 16)
POOL_HIST = 16
CONV_HIST = 8
DN_CONV = 4
SC_CONV = 3
SGU_BLOCK = 128
FFN_CHUNK = 256
NEUMANN_STEPS = 5

FFN_ROWS = 512
MIX_ROWS = 256
AB_FRONT_PHASES_PER_STATE_PHASE = 2
MASKED_LOG = -1e30
VMEM_LIMIT = 56 * 1024 * 1024


def _dot(a, b):
    return jnp.dot(a, b, preferred_element_type=F32)


def _dot_nt(a, b):
    return lax.dot_general(a, b, (((1,), (1,)), ((), ())), preferred_element_type=F32)


def _dot_tn(a, b):
    return lax.dot_general(a, b, (((0,), (0,)), ((), ())), preferred_element_type=F32)


def _split3(x):
    hi = x.astype(BF16)
    r = x - hi.astype(F32)
    mid = r.astype(BF16)
    lo = (r - mid.astype(F32)).astype(BF16)
    return hi, mid, lo


def _dot_exact_rhs(a01, x):
    hi, mid, lo = _split3(x)
    return _dot(a01, hi) + _dot(a01, mid) + _dot(a01, lo)


def _dot_exact_lhs(x, b01):
    hi, mid, lo = _split3(x)
    return _dot(hi, b01) + _dot(mid, b01) + _dot(lo, b01)


def _rmsnorm(x, g):
    return x * lax.rsqrt(jnp.mean(x * x, axis=-1, keepdims=True) + EPS) * g


def _sigmoid(x):
    return 1.0 / (1.0 + jnp.exp(-x))


def _silu(x):
    return x * _sigmoid(x)


def _gelu_tanh(x):
    return 0.5 * x * (1.0 + jnp.tanh(0.7978845608028654 * (x + 0.044715 * (x * x * x))))


def _softplus(x):
    return jnp.maximum(x, 0.0) + jnp.log1p(jnp.exp(-jnp.abs(x)))


def _ffn_body(*refs, n_chunks, final):
    if final:
        h_ref, g_ref, wg_ref, wu_ref, wd_ref, fg_ref, o_ref = refs
    else:
        h_ref, g_ref, wg_ref, wu_ref, wd_ref, o_ref = refs
    h = h_ref[...]
    hn = _rmsnorm(h, g_ref[...]).astype(BF16)
    acc = None
    for c in range(n_chunks):
        cols = slice(c * FFN_CHUNK, (c + 1) * FFN_CHUNK)
        gate = _dot(hn, wg_ref[:, cols])
        up = _dot(hn, wu_ref[:, cols])
        act = (_silu(gate) * up).astype(BF16)
        part = _dot(act, wd_ref[cols, :])
        acc = part if acc is None else acc + part
    out = h + 0.5 * acc
    if final:
        out = _rmsnorm(out, fg_ref[...])
    o_ref[...] = out


def _const_spec(shape):
    zeros = (0,) * len(shape)
    return pl.BlockSpec(shape, lambda *_: zeros, pipeline_mode=pl.Buffered(1))


def _ffn(h, norm_g, w_gate, w_up, w_down, final_g=None):
    n, d = h.shape
    f = w_gate.shape[1]
    n_chunks = f // FFN_CHUNK
    rows = min(FFN_ROWS, n)
    wg = w_gate.astype(BF16)
    wu = w_up.astype(BF16)
    wd = w_down.astype(BF16)
    row_spec = pl.BlockSpec((rows, d), lambda i: (i, 0))
    args = [h, norm_g.reshape(1, d), wg, wu, wd]
    in_specs = [row_spec, _const_spec((1, d)), _const_spec(wg.shape), _const_spec(wu.shape),
                _const_spec(wd.shape)]
    if final_g is not None:
        args.append(final_g.reshape(1, d))
        in_specs.append(_const_spec((1, d)))
    return pl.pallas_call(
        functools.partial(_ffn_body, n_chunks=n_chunks, final=final_g is not None),
        grid=(n // rows,),
        in_specs=in_specs,
        out_specs=row_spec,
        out_shape=jax.ShapeDtypeStruct((n, d), F32),
        compiler_params=pltpu.CompilerParams(
            dimension_semantics=("parallel",), vmem_limit_bytes=VMEM_LIMIT),
        name="ffn_final" if final_g is not None else "ffn",
    )(*args)


def _ab_stage_front(h, t, slot, masks, g_ref, w_main_ref, w_bg_ref, pool_w_ref, pool_scale_ref,
                    conv_w_ref, a_log_ref, dt_bias_ref, expand_ref, a_ext, qkv_ext, stage,
                    *, ts, pool_width, dn_width, heads):
    hd = LANES
    n_chunks = ts // CHUNK
    tril, strict, tril01, eye = masks
    st_u, st_w16, st_qd16, st_aqk, st_kukw, st_dec, st_ya, st_gate = stage
    hn = _rmsnorm(h, g_ref[...]).astype(BF16)
    proj = _dot(hn, w_main_ref[...])
    bgp = _dot(hn, w_bg_ref[...])
    yield

    a_in = proj[:, :pool_width]
    a_ext[POOL_HIST:POOL_HIST + ts, :] = a_in
    pos = t * ts + lax.broadcasted_iota(jnp.int32, (ts, LANES), 0)
    y_parts = []
    for gi, win in enumerate(POOL_WINDOWS):
        cols = slice(gi * LANES, (gi + 1) * LANES)
        cur = a_in[:, cols]
        s = cur
        for j in range(1, win):
            s = s + a_ext[POOL_HIST - j:POOL_HIST - j + ts, cols]
        cnt = jnp.minimum(pos + 1, win).astype(F32)
        pooled = s / cnt - cur
        mixed = _dot(pooled.astype(BF16), pool_w_ref[gi])
        y_parts.append(mixed * pool_scale_ref[:, cols])
    a_ext[0:POOL_HIST, :] = a_ext[ts:ts + POOL_HIST, :]
    st_ya[slot] = jnp.concatenate(y_parts, axis=1).astype(BF16)
    yield

    o0 = pool_width
    qkv_ext[CONV_HIST:CONV_HIST + ts, :] = proj[:, o0:o0 + 3 * dn_width]
    conv = conv_w_ref[DN_CONV - 1:DN_CONV, :] * qkv_ext[CONV_HIST:CONV_HIST + ts, :]
    for j in range(DN_CONV - 1):
        lo = CONV_HIST - (DN_CONV - 1) + j
        conv = conv + conv_w_ref[j:j + 1, :] * qkv_ext[lo:lo + ts, :]
    qkv_ext[0:CONV_HIST, :] = qkv_ext[ts:ts + CONV_HIST, :]
    qkv = _silu(conv)
    st_gate[slot] = _silu(proj[:, o0 + 3 * dn_width:o0 + 4 * dn_width])
    yield

    lane = lax.broadcasted_iota(jnp.int32, (ts, LANES), 1)
    beta_all = _sigmoid(bgp)
    g_all = -jnp.exp(a_log_ref[...]) * _softplus(bgp + dt_bias_ref[...])
    gc_n = _dot_exact_rhs(tril01, jnp.where(lane < heads, beta_all, g_all))
    gates = _dot_exact_lhs(jnp.where(lane < heads, beta_all, gc_n), expand_ref[...])
    beta_x = gates[:, :dn_width]
    gc_x = gates[:, dn_width:]
    gc_t = jnp.transpose(gc_n)
    yield

    ks, gcs, invs, l16s, rhss = ([] for _ in range(5))
    for hh in range(heads):
        hs = slice(hh * hd, (hh + 1) * hd)
        q = qkv[:, hs]
        k = qkv[:, dn_width + hh * hd:dn_width + (hh + 1) * hd]
        v = qkv[:, 2 * dn_width + hh * hd:2 * dn_width + (hh + 1) * hd]
        q = q * lax.rsqrt(jnp.sum(q * q, axis=-1, keepdims=True) + EPS) * (hd ** -0.5)
        k = k * lax.rsqrt(jnp.sum(k * k, axis=-1, keepdims=True) + EPS)
        beta = beta_x[:, hs]
        gc = gc_x[:, hs]
        eg = jnp.exp(gc)
        gc_wide = jnp.concatenate([gc] * (ts // hd), axis=1)
        dm = gc_wide - gc_t[heads + hh:heads + hh + 1, :]
        gamma = jnp.exp(jnp.where(tril, dm, MASKED_LOG))
        kb = k * beta
        k16 = k.astype(BF16)
        lmat = jnp.where(strict, _dot_nt(kb.astype(BF16), k16) * gamma, 0.0)
        ks.append(k)
        gcs.append(gc)
        l16s.append(lmat.astype(BF16))
        invs.append(eye - lmat)
        rhss.append(jnp.concatenate([v * beta, kb * eg], axis=1).astype(BF16))
        st_aqk[slot, hh] = (_dot_nt(q.astype(BF16), k16) * gamma).astype(BF16)
        st_qd16[slot, hh] = (q * eg).astype(BF16)
        if hh % 2 == 1:
            yield

    pws = [_dot(l16s[i], l16s[i]) for i in range(heads)]
    yield
    for step in range(NEUMANN_STEPS):
        p16s = [pws[i].astype(BF16) for i in range(heads)]
        invs = [invs[i] + _dot(invs[i].astype(BF16), p16s[i]) for i in range(heads)]
        if step + 1 < NEUMANN_STEPS:
            pws = [_dot(p16s[i], p16s[i]) for i in range(heads)]
        yield

    for hh in range(heads):
        sol = _dot(invs[hh].astype(BF16), rhss[hh])
        sol16 = sol.astype(BF16)
        st_u[slot, hh] = sol[:, :hd]
        st_w16[slot, hh] = sol16[:, hd:]
        for c in range(n_chunks):
            rows = slice(c * CHUNK, (c + 1) * CHUNK)
            g_last = gcs[hh][(c + 1) * CHUNK - 1:(c + 1) * CHUNK, :]
            k_dec = (ks[hh][rows] * jnp.exp(g_last - gcs[hh][rows])).astype(BF16)
            st_kukw[slot, hh, c] = _dot_tn(k_dec, sol16[rows])
            st_dec[slot, hh, c] = jnp.broadcast_to(jnp.exp(g_last), (8, hd))
    yield


def _ab_stage_state(h, slot, out_norm_ref, w_out_ref, o_ref, state, stage, *, ts, heads):
    hd = LANES
    n_chunks = ts // CHUNK
    st_u, st_w16, st_qd16, st_aqk, st_kukw, st_dec, st_ya, st_gate = stage
    s_cur = [state[hh] for hh in range(heads)]
    vn_parts = [[] for _ in range(heads)]
    o_parts = [[] for _ in range(heads)]
    for c in range(n_chunks):
        rows = pl.ds(c * CHUNK, CHUNK)
        for hh in range(heads):
            s16 = s_cur[hh].astype(BF16)
            wq = jnp.concatenate([st_w16[slot, hh, rows, :], st_qd16[slot, hh, rows, :]], axis=0)
            ws_qs = _dot(wq, s16)
            v_new = st_u[slot, hh, rows, :] - ws_qs[:CHUNK]
            vn_parts[hh].append(v_new.astype(BF16))
            pad = [jnp.zeros((CHUNK, hd), BF16)] * (n_chunks - c - 1)
            vn_buf = jnp.concatenate(vn_parts[hh] + pad, axis=0)
            o_parts[hh].append(ws_qs[CHUNK:] + _dot(st_aqk[slot, hh, rows, :], vn_buf))
            ku_kw = st_kukw[slot, hh, c]
            s_cur[hh] = (s_cur[hh] * st_dec[slot, hh, c][0:1, :] + ku_kw[:, :hd]
                         - _dot(ku_kw[:, hd:].astype(BF16), s16))
        yield
    for hh in range(heads):
        state[hh] = s_cur[hh]
    gate = st_gate[slot]
    o_heads = []
    for hh in range(heads):
        o = jnp.concatenate(o_parts[hh], axis=0)
        o_heads.append((_rmsnorm(o, out_norm_ref[...]) * gate[:, hh * hd:(hh + 1) * hd]).astype(BF16))
    y = jnp.concatenate([st_ya[slot]] + o_heads, axis=1)
    o_ref[0] = h + _dot(y, w_out_ref[...])
    yield


def _mixer_ab_body(h_front_ref, h_state_ref, g_ref, w_main_ref, w_bg_ref, pool_w_ref,
                   pool_scale_ref, conv_w_ref, a_log_ref, dt_bias_ref, out_norm_ref, expand_ref,
                   w_out_ref, o_ref, a_ext, qkv_ext, state, *stage,
                   ts, tiles_per_row, pool_width, dn_width, heads):
    g = pl.program_id(0)
    t_front = lax.rem(g, tiles_per_row)
    t_state = lax.rem(g + tiles_per_row - 1, tiles_per_row)
    slot_front = lax.rem(g, 2)
    slot_state = 1 - slot_front

    @pl.when(g == 0)
    def _():
        for ref in stage:
            ref[...] = jnp.zeros(ref.shape, ref.dtype)

    @pl.when(t_front == 0)
    def _():
        a_ext[0:POOL_HIST, :] = jnp.zeros((POOL_HIST, pool_width), F32)
        qkv_ext[0:CONV_HIST, :] = jnp.zeros((CONV_HIST, 3 * dn_width), F32)

    @pl.when((t_state == 0) | (g == 0))
    def _():
        state[...] = jnp.zeros(state.shape, F32)

    ri = lax.broadcasted_iota(jnp.int32, (ts, ts), 0)
    ci = lax.broadcasted_iota(jnp.int32, (ts, ts), 1)
    same = (ri ^ ci) < CHUNK
    tril = same & (ci <= ri)
    strict = same & (ci < ri)
    masks = (tril, strict, jnp.where(tril, 1.0, 0.0).astype(BF16), jnp.where(ri == ci, 1.0, 0.0))

    front = _ab_stage_front(
        h_front_ref[0], t_front, slot_front, masks, g_ref, w_main_ref, w_bg_ref, pool_w_ref,
        pool_scale_ref, conv_w_ref, a_log_ref, dt_bias_ref, expand_ref, a_ext, qkv_ext, stage,
        ts=ts, pool_width=pool_width, dn_width=dn_width, heads=heads)
    back = _ab_stage_state(h_state_ref[0], slot_state, out_norm_ref, w_out_ref, o_ref, state,
                           stage, ts=ts, heads=heads)
    front_phases = 0
    back_done = False
    for _ in front:
        front_phases += 1
        if front_phases % AB_FRONT_PHASES_PER_STATE_PHASE == 0 and not back_done:
            back_done = next(back, "done") == "done"
    for _ in back:
        pass


def _mixer_ab(h, norm_g, w_in, pool_w, pool_scale, conv_w, a_log, dt_bias, out_norm, w_out):
    b, s, d = h.shape
    heads = a_log.shape[0]
    pool_width = pool_scale.shape[0]
    dn_width = heads * LANES
    main = pool_width + 4 * dn_width
    ts = min(MIX_ROWS, s)
    tiles_per_row = s // ts
    n_tiles = b * tiles_per_row
    n_chunks = ts // CHUNK
    w_main = w_in[:, :main].astype(BF16)
    w_bg = jnp.pad(w_in[:, main:main + 2 * heads], ((0, 0), (0, LANES - 2 * heads))).astype(BF16)
    gate_row = lambda p: jnp.pad(p, (heads, LANES - 2 * heads)).reshape(1, LANES)
    r_idx = jnp.arange(LANES)[:, None]
    c_idx = jnp.arange(2 * dn_width)[None, :]
    expand = (r_idx == c_idx // LANES).astype(BF16)
    tiles = h.reshape(n_tiles, ts, d)
    front_spec = pl.BlockSpec((1, ts, d), lambda g: (jnp.minimum(g, n_tiles - 1), 0, 0))
    state_spec = pl.BlockSpec((1, ts, d), lambda g: (jnp.maximum(g - 1, 0), 0, 0))
    consts = [norm_g.reshape(1, d), w_main, w_bg, pool_w.astype(BF16),
              pool_scale.reshape(1, pool_width), conv_w, gate_row(a_log), gate_row(dt_bias),
              out_norm.reshape(1, LANES), expand, w_out.astype(BF16)]
    stage_shapes = [
        pltpu.VMEM((2, heads, ts, LANES), F32),
        pltpu.VMEM((2, heads, ts, LANES), BF16),
        pltpu.VMEM((2, heads, ts, LANES), BF16),
        pltpu.VMEM((2, heads, ts, ts), BF16),
        pltpu.VMEM((2, heads, n_chunks, LANES, 2 * LANES), F32),
        pltpu.VMEM((2, heads, n_chunks, 8, LANES), F32),
        pltpu.VMEM((2, ts, pool_width), BF16),
        pltpu.VMEM((2, ts, dn_width), F32),
    ]
    out = pl.pallas_call(
        functools.partial(_mixer_ab_body, ts=ts, tiles_per_row=tiles_per_row,
                          pool_width=pool_width, dn_width=dn_width, heads=heads),
        grid=(n_tiles + 1,),
        in_specs=[front_spec, state_spec] + [_const_spec(a.shape) for a in consts],
        out_specs=state_spec,
        out_shape=jax.ShapeDtypeStruct((n_tiles, ts, d), F32),
        scratch_shapes=[pltpu.VMEM((POOL_HIST + ts, pool_width), F32),
                        pltpu.VMEM((CONV_HIST + ts, 3 * dn_width), F32),
                        pltpu.VMEM((heads, LANES, LANES), F32)] + stage_shapes,
        compiler_params=pltpu.CompilerParams(
            dimension_semantics=("arbitrary",), vmem_limit_bytes=VMEM_LIMIT),
        name="mixer_ab",
    )(tiles, tiles, *consts)
    return out.reshape(b, s, d)


def _mixer_cd_body(h_ref, g_ref, w_in_ref, ln_g_ref, ln_b_ref, sgu_w_ref, sgu_bias_ref,
                   conv_w_ref, w_out_ref, o_ref, p_ext, *, ts, sgu_width, sc_width, heads):
    t = pl.program_id(1)
    n_blocks = ts // SGU_BLOCK

    @pl.when(t == 0)
    def _():
        p_ext[0:CONV_HIST, :] = jnp.zeros((CONV_HIST, sc_width), F32)

    h = h_ref[0]
    hn = _rmsnorm(h, g_ref[...]).astype(BF16)
    proj = _dot(hn, w_in_ref[...])
    uv = _gelu_tanh(proj[:, :2 * sgu_width])
    u = uv[:, :sgu_width]
    v = uv[:, sgu_width:]
    mu = jnp.mean(v, axis=-1, keepdims=True)
    vc = v - mu
    v = vc * lax.rsqrt(jnp.mean(vc * vc, axis=-1, keepdims=True) + EPS) * ln_g_ref[...] + ln_b_ref[...]
    v16 = v.astype(BF16)

    ri = lax.broadcasted_iota(jnp.int32, (SGU_BLOCK, SGU_BLOCK), 0)
    ci = lax.broadcasted_iota(jnp.int32, (SGU_BLOCK, SGU_BLOCK), 1)
    head_cols = []
    for hh in range(heads):
        hs = slice(hh * LANES, (hh + 1) * LANES)
        ws = jnp.where(ci <= ri, sgu_w_ref[hh], 0.0).astype(BF16)
        rhs = jnp.concatenate([v16[n * SGU_BLOCK:(n + 1) * SGU_BLOCK, hs] for n in range(n_blocks)],
                              axis=1)
        m = _dot(ws, rhs)
        head_cols.append(jnp.concatenate(
            [m[:, n * LANES:(n + 1) * LANES] for n in range(n_blocks)], axis=0))
    mixed = jnp.concatenate(head_cols, axis=1)
    bias = jnp.concatenate([sgu_bias_ref[...]] * n_blocks, axis=0)
    y_c = u * (mixed + bias)

    o0 = 2 * sgu_width
    xd = proj[:, o0:o0 + sc_width]
    bg = proj[:, o0 + sc_width:o0 + 2 * sc_width]
    cg = proj[:, o0 + 2 * sc_width:o0 + 3 * sc_width]
    p_ext[CONV_HIST:CONV_HIST + ts, :] = cg * xd
    conv = conv_w_ref[SC_CONV - 1:SC_CONV, :] * p_ext[CONV_HIST:CONV_HIST + ts, :]
    for j in range(SC_CONV - 1):
        lo = CONV_HIST - (SC_CONV - 1) + j
        conv = conv + conv_w_ref[j:j + 1, :] * p_ext[lo:lo + ts, :]
    p_ext[0:CONV_HIST, :] = p_ext[ts:ts + CONV_HIST, :]
    y_d = bg * conv

    y = jnp.concatenate([y_c, y_d], axis=1).astype(BF16)
    o_ref[0] = h + _dot(y, w_out_ref[...])


def _mixer_cd(h, norm_g, w_in, ln_g, ln_b, sgu_w, sgu_bias, conv_w, w_out):
    b, s, d = h.shape
    heads = sgu_w.shape[0]
    sgu_width = ln_g.shape[0]
    sc_width = conv_w.shape[1]
    ts = min(MIX_ROWS, s)
    bias_x = jnp.repeat(sgu_bias.T, LANES, axis=1)
    tok_spec = pl.BlockSpec((1, ts, d), lambda i, j: (i, j, 0))
    args = [h, norm_g.reshape(1, d), w_in.astype(BF16), ln_g.reshape(1, sgu_width),
            ln_b.reshape(1, sgu_width), sgu_w, bias_x, conv_w, w_out.astype(BF16)]
    in_specs = [tok_spec] + [_const_spec(a.shape) for a in args[1:]]
    return pl.pallas_call(
        functools.partial(_mixer_cd_body, ts=ts, sgu_width=sgu_width, sc_width=sc_width,
                          heads=heads),
        grid=(b, s // ts),
        in_specs=in_specs,
        out_specs=tok_spec,
        out_shape=jax.ShapeDtypeStruct((b, s, d), F32),
        scratch_shapes=[pltpu.VMEM((CONV_HIST + ts, sc_width), F32)],
        compiler_params=pltpu.CompilerParams(
            dimension_semantics=("arbitrary", "arbitrary"), vmem_limit_bytes=VMEM_LIMIT),
        name="mixer_cd",
    )(*args)


def kernel(x, ffn1_norm, ffn1_w_gate, ffn1_w_up, ffn1_w_down, mix_norm, ffn2_norm, ffn2_w_gate, ffn2_w_up, ffn2_w_down, ab_w_in, pool_w, pool_scale, dn_conv_w, dn_a_log, dn_dt_bias, dn_out_norm, ab_w_out, cd_w_in, sgu_norm_g, sgu_norm_b, sgu_w, sgu_bias, sc_conv_w, cd_w_out, final_norm):
    b, s, d = x.shape
    depth = ffn1_norm.shape[0]
    h = x.reshape(b * s, d)
    for layer in range(depth):
        h = _ffn(h, ffn1_norm[layer], ffn1_w_gate[layer], ffn1_w_up[layer], ffn1_w_down[layer])
        h3 = h.reshape(b, s, d)
        if layer % 2 == 0:
            e = layer // 2
            h3 = _mixer_ab(h3, mix_norm[layer], ab_w_in[e], pool_w[e], pool_scale[e], dn_conv_w[e],
                           dn_a_log[e], dn_dt_bias[e], dn_out_norm[e], ab_w_out[e])
        else:
            o = layer // 2
            h3 = _mixer_cd(h3, mix_norm[layer], cd_w_in[o], sgu_norm_g[o], sgu_norm_b[o], sgu_w[o],
                           sgu_bias[o], sc_conv_w[o], cd_w_out[o])
        h = h3.reshape(b * s, d)
        h = _ffn(h, ffn2_norm[layer], ffn2_w_gate[layer], ffn2_w_up[layer], ffn2_w_down[layer],
                 final_g=final_norm if layer == depth - 1 else None)
    return h.reshape(b, s, d)
```

```python
import functools

import jax
import jax.numpy as jnp
from jax import lax
from jax.experimental import pallas as pl
from jax.experimental.pallas import tpu as pltpu

F32 = jnp.float32
BF16 = jnp.bfloat16
EPS = 1e-6

LANES = 128
CHUNK = 64
POOL_WINDOWS = (2, 4, 8, 16)
POOL_HIST = 16
CONV_HIST = 8
DN_CONV = 4
SC_CONV = 3
SGU_BLOCK = 128
FFN_CHUNK = 256
NEUMANN_STEPS = 5

FFN_ROWS = 1024
MIX_ROWS = 512
AB_EMISSION_ORDER = "FBB" * 8
PAIR = 2 * CHUNK
MASKED_LOG = -1e30
VMEM_LIMIT = 56 * 1024 * 1024


def _dot(a, b):
    return jnp.dot(a, b, preferred_element_type=F32)


def _dot_nt(a, b):
    return lax.dot_general(a, b, (((1,), (1,)), ((), ())), preferred_element_type=F32)


def _dot_tn(a, b):
    return lax.dot_general(a, b, (((0,), (0,)), ((), ())), preferred_element_type=F32)


def _split3(x):
    hi = x.astype(BF16)
    r = x - hi.astype(F32)
    mid = r.astype(BF16)
    lo = (r - mid.astype(F32)).astype(BF16)
    return hi, mid, lo


def _dot_exact_rhs(a01, x):
    hi, mid, lo = _split3(x)
    return _dot(a01, hi) + _dot(a01, mid) + _dot(a01, lo)


def _rmsnorm(x, g):
    return x * lax.rsqrt(jnp.mean(x * x, axis=-1, keepdims=True) + EPS) * g


def _sigmoid(x):
    return 1.0 / (1.0 + jnp.exp(-x))


def _silu(x):
    half = 0.5 * x
    return half + half * jnp.tanh(half)


def _gelu_tanh(x):
    return 0.5 * x * (1.0 + jnp.tanh(0.7978845608028654 * (x + 0.044715 * (x * x * x))))


def _softplus(x):
    return jnp.maximum(x, 0.0) + jnp.log1p(jnp.exp(-jnp.abs(x)))


def _ffn_body(*refs, n_chunks, final):
    if final:
        h_ref, g_ref, wg_ref, wu_ref, wd_ref, fg_ref, o_ref = refs
    else:
        h_ref, g_ref, wg_ref, wu_ref, wd_ref, o_ref = refs
    h = h_ref[...]
    hn = _rmsnorm(h, g_ref[...]).astype(BF16)
    acc = None
    for c in range(n_chunks):
        cols = slice(c * FFN_CHUNK, (c + 1) * FFN_CHUNK)
        gate = _dot(hn, wg_ref[:, cols])
        up = _dot(hn, wu_ref[:, cols])
        act = (_silu(gate) * up).astype(BF16)
        part = _dot(act, wd_ref[cols, :])
        acc = part if acc is None else acc + part
    out = h + 0.5 * acc
    if final:
        out = _rmsnorm(out, fg_ref[...])
    o_ref[...] = out


def _const_spec(shape):
    zeros = (0,) * len(shape)
    return pl.BlockSpec(shape, lambda *_: zeros, pipeline_mode=pl.Buffered(1))


def _ffn(h, norm_g, w_gate, w_up, w_down, final_g=None):
    n, d = h.shape
    f = w_gate.shape[1]
    n_chunks = f // FFN_CHUNK
    rows = min(FFN_ROWS, n)
    wg = w_gate.astype(BF16)
    wu = w_up.astype(BF16)
    wd = w_down.astype(BF16)
    row_spec = pl.BlockSpec((rows, d), lambda i: (i, 0))
    args = [h, norm_g.reshape(1, d), wg, wu, wd]
    in_specs = [row_spec, _const_spec((1, d)), _const_spec(wg.shape), _const_spec(wu.shape),
                _const_spec(wd.shape)]
    if final_g is not None:
        args.append(final_g.reshape(1, d))
        in_specs.append(_const_spec((1, d)))
    return pl.pallas_call(
        functools.partial(_ffn_body, n_chunks=n_chunks, final=final_g is not None),
        grid=(n // rows,),
        in_specs=in_specs,
        out_specs=row_spec,
        out_shape=jax.ShapeDtypeStruct((n, d), F32),
        compiler_params=pltpu.CompilerParams(
            dimension_semantics=("parallel",), vmem_limit_bytes=VMEM_LIMIT),
        name="ffn_final" if final_g is not None else "ffn",
    )(*args)


def _ab_stage_front(h, t, slot, masks, g_ref, w_main_ref, w_bg_ref, pool_w_ref, pool_scale_ref,
                    conv_w_ref, a_log_ref, dt_bias_ref, a_ext, qkv_ext, stage,
                    *, ts, pool_width, dn_width, heads):
    hd = LANES
    n_chunks = ts // CHUNK
    n_pairs = ts // PAIR
    tril, strict, tril01, _ = masks
    st_l, st_rhs, st_kdec, st_dec, st_aqk, st_qd16, st_ya, st_gate = stage
    hn = _rmsnorm(h, g_ref[...]).astype(BF16)
    proj = _dot(hn, w_main_ref[...])
    bgp = _dot(hn, w_bg_ref[...])
    yield

    a_in = proj[:, :pool_width]
    a_ext[POOL_HIST:POOL_HIST + ts, :] = a_in
    pos = t * ts + lax.broadcasted_iota(jnp.int32, (ts, LANES), 0)
    y_parts = []
    for gi, win in enumerate(POOL_WINDOWS):
        cols = slice(gi * LANES, (gi + 1) * LANES)
        cur = a_in[:, cols]
        s = cur
        for j in range(1, win):
            s = s + a_ext[POOL_HIST - j:POOL_HIST - j + ts, cols]
        cnt = jnp.minimum(pos + 1, win).astype(F32)
        pooled = s / cnt - cur
        mixed = _dot(pooled.astype(BF16), pool_w_ref[gi])
        y_parts.append(mixed * pool_scale_ref[:, cols])
    a_ext[0:POOL_HIST, :] = a_ext[ts:ts + POOL_HIST, :]
    st_ya[slot] = jnp.concatenate(y_parts, axis=1).astype(BF16)
    yield

    o0 = pool_width
    qkv_ext[CONV_HIST:CONV_HIST + ts, :] = proj[:, o0:o0 + 3 * dn_width]
    conv = conv_w_ref[DN_CONV - 1:DN_CONV, :] * qkv_ext[CONV_HIST:CONV_HIST + ts, :]
    for j in range(DN_CONV - 1):
        lo = CONV_HIST - (DN_CONV - 1) + j
        conv = conv + conv_w_ref[j:j + 1, :] * qkv_ext[lo:lo + ts, :]
    qkv_ext[0:CONV_HIST, :] = qkv_ext[ts:ts + CONV_HIST, :]
    qkv = _silu(conv)
    st_gate[slot] = _silu(proj[:, o0 + 3 * dn_width:o0 + 4 * dn_width])
    yield

    lane = lax.broadcasted_iota(jnp.int32, (ts, LANES), 1)
    beta_all = _sigmoid(bgp)
    g_all = -jnp.exp(a_log_ref[...]) * _softplus(bgp + dt_bias_ref[...])
    chunk_tril01 = jnp.concatenate(
        [jnp.concatenate([tril01 if q == p else jnp.zeros((PAIR, PAIR), BF16) for q in range(n_pairs)],
                         axis=1) for p in range(n_pairs)], axis=0)
    gc_n = _dot_exact_rhs(chunk_tril01, jnp.where(lane < heads, beta_all, g_all))
    beta_x = [jnp.broadcast_to(beta_all[:, hh:hh + 1], (ts, LANES)) for hh in range(heads)]
    gc_x = [jnp.broadcast_to(gc_n[:, heads + hh:heads + hh + 1], (ts, LANES)) for hh in range(heads)]
    gc_t = jnp.transpose(gc_n)
    yield

    for hh in range(heads):
        hs = slice(hh * hd, (hh + 1) * hd)
        q = qkv[:, hs]
        k = qkv[:, dn_width + hh * hd:dn_width + (hh + 1) * hd]
        v = qkv[:, 2 * dn_width + hh * hd:2 * dn_width + (hh + 1) * hd]
        q = q * lax.rsqrt(jnp.sum(q * q, axis=-1, keepdims=True) + EPS) * (hd ** -0.5)
        k = k * lax.rsqrt(jnp.sum(k * k, axis=-1, keepdims=True) + EPS)
        beta = beta_x[hh]
        gc = gc_x[hh]
        eg = jnp.exp(gc)
        kb = k * beta
        k16 = k.astype(BF16)
        kb16 = kb.astype(BF16)
        q16 = q.astype(BF16)
        st_rhs[slot, hh] = jnp.concatenate([v * beta, kb * eg], axis=1).astype(BF16)
        st_qd16[slot, hh] = (q * eg).astype(BF16)
        for p in range(n_pairs):
            rows = slice(p * PAIR, (p + 1) * PAIR)
            dm = gc[rows] - gc_t[heads + hh:heads + hh + 1, rows]
            gamma = jnp.exp(jnp.where(tril, dm, MASKED_LOG))
            st_l[slot, hh, p] = jnp.where(strict, _dot_nt(kb16[rows], k16[rows]) * gamma, 0.0)
            st_aqk[slot, hh, p] = (_dot_nt(q16[rows], k16[rows]) * gamma).astype(BF16)
        k_dec = []
        for c in range(n_chunks):
            rows = slice(c * CHUNK, (c + 1) * CHUNK)
            g_last = gc[(c + 1) * CHUNK - 1:(c + 1) * CHUNK, :]
            k_dec.append((k[rows] * jnp.exp(g_last - gc[rows])).astype(BF16))
            st_dec[slot, hh, c] = jnp.broadcast_to(jnp.exp(g_last), (8, hd))
        st_kdec[slot, hh] = jnp.concatenate(k_dec, axis=0)
        if hh % 2 == 1:
            yield


def _ab_stage_state(h, slot, masks, out_norm_ref, w_out_ref, o_ref, state, stage, *, ts, heads):
    hd = LANES
    n_chunks = ts // CHUNK
    n_pairs = ts // PAIR
    per_pair = PAIR // CHUNK
    eye = masks[3]
    st_l, st_rhs, st_kdec, st_dec, st_aqk, st_qd16, st_ya, st_gate = stage
    blocks = [(hh, p) for hh in range(heads) for p in range(n_pairs)]

    lmats = [st_l[slot, hh, p] for hh, p in blocks]
    l16s = [m.astype(BF16) for m in lmats]
    invs = [eye - m for m in lmats]
    pws = [_dot(m, m) for m in l16s]
    yield
    for step in range(NEUMANN_STEPS):
        p16s = [m.astype(BF16) for m in pws]
        invs = [invs[i] + _dot(invs[i].astype(BF16), p16s[i]) for i in range(len(blocks))]
        if step + 1 < NEUMANN_STEPS:
            pws = [_dot(m, m) for m in p16s]
        yield

    us, w16s, ku_kws = [], [], []
    for i, (hh, p) in enumerate(blocks):
        sol = _dot(invs[i].astype(BF16), st_rhs[slot, hh, pl.ds(p * PAIR, PAIR), :])
        sol16 = sol.astype(BF16)
        us.append(sol[:, :hd])
        w16s.append(sol16[:, hd:])
        for lc in range(per_pair):
            c = p * per_pair + lc
            k_dec = st_kdec[slot, hh, pl.ds(c * CHUNK, CHUNK), :]
            ku_kws.append(_dot_tn(k_dec, sol16[lc * CHUNK:(lc + 1) * CHUNK]))
    yield

    s_cur = [state[hh] for hh in range(heads)]
    vn_parts = [[] for _ in blocks]
    o_parts = [[] for _ in range(heads)]
    for c in range(n_chunks):
        p, lc = divmod(c, per_pair)
        local = slice(lc * CHUNK, (lc + 1) * CHUNK)
        for hh in range(heads):
            i = hh * n_pairs + p
            s16 = s_cur[hh].astype(BF16)
            wq = jnp.concatenate([w16s[i][local], st_qd16[slot, hh, pl.ds(c * CHUNK, CHUNK), :]], axis=0)
            ws_qs = _dot(wq, s16)
            v_new = us[i][local] - ws_qs[:CHUNK]
            vn_parts[i].append(v_new.astype(BF16))
            pad = [jnp.zeros((CHUNK, hd), BF16)] * (per_pair - lc - 1)
            vn_buf = jnp.concatenate(vn_parts[i] + pad, axis=0)
            aqk_rows = st_aqk[slot, hh, p, pl.ds(lc * CHUNK, CHUNK), :]
            o_parts[hh].append(ws_qs[CHUNK:] + _dot(aqk_rows, vn_buf))
            ku_kw = ku_kws[(hh * n_pairs + p) * per_pair + lc]
            s_cur[hh] = (s_cur[hh] * st_dec[slot, hh, c][0:1, :] + ku_kw[:, :hd]
                         - _dot(ku_kw[:, hd:].astype(BF16), s16))
        yield
    for hh in range(heads):
        state[hh] = s_cur[hh]
    gate = st_gate[slot]
    o_heads = []
    for hh in range(heads):
        o = jnp.concatenate(o_parts[hh], axis=0)
        o_heads.append((_rmsnorm(o, out_norm_ref[...]) * gate[:, hh * hd:(hh + 1) * hd]).astype(BF16))
    y = jnp.concatenate([st_ya[slot]] + o_heads, axis=1)
    o_ref[0] = h + _dot(y, w_out_ref[...])
    yield


def _mixer_ab_body(h_front_ref, h_state_ref, g_ref, w_main_ref, w_bg_ref, pool_w_ref,
                   pool_scale_ref, conv_w_ref, a_log_ref, dt_bias_ref, out_norm_ref,
                   w_out_ref, o_ref, a_ext, qkv_ext, state, *stage,
                   ts, tiles_per_row, pool_width, dn_width, heads):
    g = pl.program_id(0)
    t_front = lax.rem(g, tiles_per_row)
    t_state = lax.rem(g + tiles_per_row - 1, tiles_per_row)
    slot_front = lax.rem(g, 2)
    slot_state = 1 - slot_front

    @pl.when(g == 0)
    def _():
        for ref in stage:
            ref[...] = jnp.zeros(ref.shape, ref.dtype)

    @pl.when(t_front == 0)
    def _():
        a_ext[0:POOL_HIST, :] = jnp.zeros((POOL_HIST, pool_width), F32)
        qkv_ext[0:CONV_HIST, :] = jnp.zeros((CONV_HIST, 3 * dn_width), F32)

    @pl.when((t_state == 0) | (g == 0))
    def _():
        state[...] = jnp.zeros(state.shape, F32)

    ri = lax.broadcasted_iota(jnp.int32, (PAIR, PAIR), 0)
    ci = lax.broadcasted_iota(jnp.int32, (PAIR, PAIR), 1)
    same = (ri ^ ci) < CHUNK
    tril = same & (ci <= ri)
    strict = same & (ci < ri)
    masks = (tril, strict, jnp.where(tril, 1.0, 0.0).astype(BF16), jnp.where(ri == ci, 1.0, 0.0))

    front = _ab_stage_front(
        h_front_ref[0], t_front, slot_front, masks, g_ref, w_main_ref, w_bg_ref, pool_w_ref,
        pool_scale_ref, conv_w_ref, a_log_ref, dt_bias_ref, a_ext, qkv_ext, stage,
        ts=ts, pool_width=pool_width, dn_width=dn_width, heads=heads)
    back = _ab_stage_state(h_state_ref[0], slot_state, masks, out_norm_ref, w_out_ref, o_ref,
                           state, stage, ts=ts, heads=heads)
    for who in AB_EMISSION_ORDER:
        next(back if who == "B" else front, None)
    for gen in (back, front):
        for _ in gen:
            pass


def _mixer_ab(h, norm_g, w_in, pool_w, pool_scale, conv_w, a_log, dt_bias, out_norm, w_out):
    b, s, d = h.shape
    heads = a_log.shape[0]
    pool_width = pool_scale.shape[0]
    dn_width = heads * LANES
    main = pool_width + 4 * dn_width
    ts = min(MIX_ROWS, s)
    tiles_per_row = s // ts
    n_tiles = b * tiles_per_row
    n_chunks = ts // CHUNK
    n_pairs = ts // PAIR
    w_main = w_in[:, :main].astype(BF16)
    w_bg = jnp.pad(w_in[:, main:main + 2 * heads], ((0, 0), (0, LANES - 2 * heads))).astype(BF16)
    gate_row = lambda p: jnp.pad(p, (heads, LANES - 2 * heads)).reshape(1, LANES)
    tiles = h.reshape(n_tiles, ts, d)
    front_spec = pl.BlockSpec((1, ts, d), lambda g: (jnp.minimum(g, n_tiles - 1), 0, 0))
    state_spec = pl.BlockSpec((1, ts, d), lambda g: (jnp.maximum(g - 1, 0), 0, 0))
    consts = [norm_g.reshape(1, d), w_main, w_bg, pool_w.astype(BF16),
              pool_scale.reshape(1, pool_width), conv_w, gate_row(a_log), gate_row(dt_bias),
              out_norm.reshape(1, LANES), w_out.astype(BF16)]
    stage_shapes = [
        pltpu.VMEM((2, heads, n_pairs, PAIR, PAIR), F32),
        pltpu.VMEM((2, heads, ts, 2 * LANES), BF16),
        pltpu.VMEM((2, heads, ts, LANES), BF16),
        pltpu.VMEM((2, heads, n_chunks, 8, LANES), F32),
        pltpu.VMEM((2, heads, n_pairs, PAIR, PAIR), BF16),
        pltpu.VMEM((2, heads, ts, LANES), BF16),
        pltpu.VMEM((2, ts, pool_width), BF16),
        pltpu.VMEM((2, ts, dn_width), F32),
    ]
    out = pl.pallas_call(
        functools.partial(_mixer_ab_body, ts=ts, tiles_per_row=tiles_per_row,
                          pool_width=pool_width, dn_width=dn_width, heads=heads),
        grid=(n_tiles + 1,),
        in_specs=[front_spec, state_spec] + [_const_spec(a.shape) for a in consts],
        out_specs=state_spec,
        out_shape=jax.ShapeDtypeStruct((n_tiles, ts, d), F32),
        scratch_shapes=[pltpu.VMEM((POOL_HIST + ts, pool_width), F32),
                        pltpu.VMEM((CONV_HIST + ts, 3 * dn_width), F32),
                        pltpu.VMEM((heads, LANES, LANES), F32)] + stage_shapes,
        compiler_params=pltpu.CompilerParams(
            dimension_semantics=("arbitrary",), vmem_limit_bytes=VMEM_LIMIT),
        name="mixer_ab",
    )(tiles, tiles, *consts)
    return out.reshape(b, s, d)


def _mixer_cd_body(h_ref, g_ref, w_in_ref, ln_g_ref, ln_b_ref, sgu_w_ref, sgu_bias_ref,
                   conv_w_ref, w_out_ref, o_ref, p_ext, *, ts, sgu_width, sc_width, heads):
    t = pl.program_id(1)
    n_blocks = ts // SGU_BLOCK

    @pl.when(t == 0)
    def _():
        p_ext[0:CONV_HIST, :] = jnp.zeros((CONV_HIST, sc_width), F32)

    h = h_ref[0]
    hn = _rmsnorm(h, g_ref[...]).astype(BF16)
    proj = _dot(hn, w_in_ref[...])
    uv = _gelu_tanh(proj[:, :2 * sgu_width])
    u = uv[:, :sgu_width]
    v = uv[:, sgu_width:]
    mu = jnp.mean(v, axis=-1, keepdims=True)
    vc = v - mu
    v = vc * lax.rsqrt(jnp.mean(vc * vc, axis=-1, keepdims=True) + EPS) * ln_g_ref[...] + ln_b_ref[...]
    v16 = v.astype(BF16)

    ri = lax.broadcasted_iota(jnp.int32, (SGU_BLOCK, SGU_BLOCK), 0)
    ci = lax.broadcasted_iota(jnp.int32, (SGU_BLOCK, SGU_BLOCK), 1)
    head_cols = []
    for hh in range(heads):
        hs = slice(hh * LANES, (hh + 1) * LANES)
        ws = jnp.where(ci <= ri, sgu_w_ref[hh], 0.0).astype(BF16)
        rhs = jnp.concatenate([v16[n * SGU_BLOCK:(n + 1) * SGU_BLOCK, hs] for n in range(n_blocks)],
                              axis=1)
        m = _dot(ws, rhs)
        head_cols.append(jnp.concatenate(
            [m[:, n * LANES:(n + 1) * LANES] for n in range(n_blocks)], axis=0))
    mixed = jnp.concatenate(head_cols, axis=1)
    bias = jnp.concatenate([sgu_bias_ref[...]] * n_blocks, axis=0)
    y_c = u * (mixed + bias)

    o0 = 2 * sgu_width
    xd = proj[:, o0:o0 + sc_width]
    bg = proj[:, o0 + sc_width:o0 + 2 * sc_width]
    cg = proj[:, o0 + 2 * sc_width:o0 + 3 * sc_width]
    p_ext[CONV_HIST:CONV_HIST + ts, :] = cg * xd
    conv = conv_w_ref[SC_CONV - 1:SC_CONV, :] * p_ext[CONV_HIST:CONV_HIST + ts, :]
    for j in range(SC_CONV - 1):
        lo = CONV_HIST - (SC_CONV - 1) + j
        conv = conv + conv_w_ref[j:j + 1, :] * p_ext[lo:lo + ts, :]
    p_ext[0:CONV_HIST, :] = p_ext[ts:ts + CONV_HIST, :]
    y_d = bg * conv

    y = jnp.concatenate([y_c, y_d], axis=1).astype(BF16)
    o_ref[0] = h + _dot(y, w_out_ref[...])


def _mixer_cd(h, norm_g, w_in, ln_g, ln_b, sgu_w, sgu_bias, conv_w, w_out):
    b, s, d = h.shape
    heads = sgu_w.shape[0]
    sgu_width = ln_g.shape[0]
    sc_width = conv_w.shape[1]
    ts = min(MIX_ROWS, s)
    bias_x = jnp.repeat(sgu_bias.T, LANES, axis=1)
    tok_spec = pl.BlockSpec((1, ts, d), lambda i, j: (i, j, 0))
    args = [h, norm_g.reshape(1, d), w_in.astype(BF16), ln_g.reshape(1, sgu_width),
            ln_b.reshape(1, sgu_width), sgu_w, bias_x, conv_w, w_out.astype(BF16)]
    in_specs = [tok_spec] + [_const_spec(a.shape) for a in args[1:]]
    return pl.pallas_call(
        functools.partial(_mixer_cd_body, ts=ts, sgu_width=sgu_width, sc_width=sc_width,
                          heads=heads),
        grid=(b, s // ts),
        in_specs=in_specs,
        out_specs=tok_spec,
        out_shape=jax.ShapeDtypeStruct((b, s, d), F32),
        scratch_shapes=[pltpu.VMEM((CONV_HIST + ts, sc_width), F32)],
        compiler_params=pltpu.CompilerParams(
            dimension_semantics=("arbitrary", "arbitrary"), vmem_limit_bytes=VMEM_LIMIT),
        name="mixer_cd",
    )(*args)


def kernel(x, ffn1_norm, ffn1_w_gate, ffn1_w_up, ffn1_w_down, mix_norm, ffn2_norm, ffn2_w_gate, ffn2_w_up, ffn2_w_down, ab_w_in, pool_w, pool_scale, dn_conv_w, dn_a_log, dn_dt_bias, dn_out_norm, ab_w_out, cd_w_in, sgu_norm_g, sgu_norm_b, sgu_w, sgu_bias, sc_conv_w, cd_w_out, final_norm):
    b, s, d = x.shape
    depth = ffn1_norm.shape[0]
    h = x.reshape(b * s, d)
    for layer in range(depth):
        h = _ffn(h, ffn1_norm[layer], ffn1_w_gate[layer], ffn1_w_up[layer], ffn1_w_down[layer])
        h3 = h.reshape(b, s, d)
        if layer % 2 == 0:
            e = layer // 2
            h3 = _mixer_ab(h3, mix_norm[layer], ab_w_in[e], pool_w[e], pool_scale[e], dn_conv_w[e],
                           dn_a_log[e], dn_dt_bias[e], dn_out_norm[e], ab_w_out[e])
        else:
            o = layer // 2
            h3 = _mixer_cd(h3, mix_norm[layer], cd_w_in[o], sgu_norm_g[o], sgu_norm_b[o], sgu_w[o],
                           sgu_bias[o], sc_conv_w[o], cd_w_out[o])
        h = h3.reshape(b * s, d)
        h = _ffn(h, ffn2_norm[layer], ffn2_w_gate[layer], ffn2_w_up[layer], ffn2_w_down[layer],
                 final_g=final_norm if layer == depth - 1 else None)
    return h.reshape(b, s, d)
```

```python
import functools

import jax
import jax.numpy as jnp
from jax import lax
from jax.experimental import pallas as pl
from jax.experimental.pallas import tpu as pltpu

F32 = jnp.float32
BF16 = jnp.bfloat16
EPS = 1e-6

LANES = 128
CHUNK = 64
POOL_WINDOWS = (2, 4, 8, 16)
POOL_HIST = 16
CONV_HIST = 8
DN_CONV = 4
SC_CONV = 3
SGU_BLOCK = 128
FFN_CHUNK = 256
NEUMANN_STEPS = 5

FFN_ROWS = 1024
MIX_ROWS = 512
AB_EMISSION_ORDER = "FFFFBBBBFBFBBFFBFFBFFBFFBFBFBFBBB"
PAIR = 2 * CHUNK
MASKED_LOG = -1e30
VMEM_LIMIT = 56 * 1024 * 1024


def _dot(a, b):
    return jnp.dot(a, b, preferred_element_type=F32)


def _dot_nt(a, b):
    return lax.dot_general(a, b, (((1,), (1,)), ((), ())), preferred_element_type=F32)


def _dot_tn(a, b):
    return lax.dot_general(a, b, (((0,), (0,)), ((), ())), preferred_element_type=F32)


def _split3(x):
    hi = x.astype(BF16)
    r = x - hi.astype(F32)
    mid = r.astype(BF16)
    lo = (r - mid.astype(F32)).astype(BF16)
    return hi, mid, lo


def _dot_exact_rhs(a01, x):
    hi, mid, lo = _split3(x)
    return _dot(a01, hi) + _dot(a01, mid) + _dot(a01, lo)


def _rmsnorm(x, g):
    return x * lax.rsqrt(jnp.mean(x * x, axis=-1, keepdims=True) + EPS) * g


def _sigmoid(x):
    return 1.0 / (1.0 + jnp.exp(-x))


def _silu(x):
    half = 0.5 * x
    return half + half * jnp.tanh(half)


def _gelu_tanh(x):
    return 0.5 * x * (1.0 + jnp.tanh(0.7978845608028654 * (x + 0.044715 * (x * x * x))))


def _softplus(x):
    return jnp.maximum(x, 0.0) + jnp.log1p(jnp.exp(-jnp.abs(x)))


def _ffn_body(*refs, n_chunks, final):
    if final:
        h_ref, g_ref, wg_ref, wu_ref, wd_ref, fg_ref, o_ref = refs
    else:
        h_ref, g_ref, wg_ref, wu_ref, wd_ref, o_ref = refs
    h = h_ref[...]
    hn = _rmsnorm(h, g_ref[...]).astype(BF16)
    acc = None
    for c in range(n_chunks):
        cols = slice(c * FFN_CHUNK, (c + 1) * FFN_CHUNK)
        gate = _dot(hn, wg_ref[:, cols])
        up = _dot(hn, wu_ref[:, cols])
        act = (_silu(gate) * up).astype(BF16)
        part = _dot(act, wd_ref[cols, :])
        acc = part if acc is None else acc + part
    out = h + 0.5 * acc
    if final:
        out = _rmsnorm(out, fg_ref[...])
    o_ref[...] = out


def _const_spec(shape):
    zeros = (0,) * len(shape)
    return pl.BlockSpec(shape, lambda *_: zeros, pipeline_mode=pl.Buffered(1))


def _ffn(h, norm_g, w_gate, w_up, w_down, final_g=None):
    n, d = h.shape
    f = w_gate.shape[1]
    n_chunks = f // FFN_CHUNK
    rows = min(FFN_ROWS, n)
    wg = w_gate.astype(BF16)
    wu = w_up.astype(BF16)
    wd = w_down.astype(BF16)
    row_spec = pl.BlockSpec((rows, d), lambda i: (i, 0))
    args = [h, norm_g.reshape(1, d), wg, wu, wd]
    in_specs = [row_spec, _const_spec((1, d)), _const_spec(wg.shape), _const_spec(wu.shape),
                _const_spec(wd.shape)]
    if final_g is not None:
        args.append(final_g.reshape(1, d))
        in_specs.append(_const_spec((1, d)))
    return pl.pallas_call(
        functools.partial(_ffn_body, n_chunks=n_chunks, final=final_g is not None),
        grid=(n // rows,),
        in_specs=in_specs,
        out_specs=row_spec,
        out_shape=jax.ShapeDtypeStruct((n, d), F32),
        compiler_params=pltpu.CompilerParams(
            dimension_semantics=("parallel",), vmem_limit_bytes=VMEM_LIMIT),
        name="ffn_final" if final_g is not None else "ffn",
    )(*args)


def _ab_stage_front(h, t, slot, masks, g_ref, w_main_ref, w_bg_ref, pool_w_ref, pool_scale_ref,
                    conv_w_ref, a_log_ref, dt_bias_ref, a_ext, qkv_ext, stage,
                    *, ts, pool_width, dn_width, heads):
    hd = LANES
    n_chunks = ts // CHUNK
    n_pairs = ts // PAIR
    tril, strict, tril01, _ = masks
    st_l, st_rhs, st_kdec, st_dec, st_aqk, st_qd16, st_ya, st_gate = stage
    o0 = pool_width
    hn = _rmsnorm(h, g_ref[...]).astype(BF16)
    bgp = _dot(hn, w_bg_ref[...])
    a_in = _dot(hn, w_main_ref[:, 0:pool_width])
    yield
    for part in range(3):
        cols = slice(o0 + part * dn_width, o0 + (part + 1) * dn_width)
        qkv_ext[CONV_HIST:CONV_HIST + ts, part * dn_width:(part + 1) * dn_width] = _dot(hn, w_main_ref[:, cols])
        yield

    conv = conv_w_ref[DN_CONV - 1:DN_CONV, :] * qkv_ext[CONV_HIST:CONV_HIST + ts, :]
    for j in range(DN_CONV - 1):
        lo = CONV_HIST - (DN_CONV - 1) + j
        conv = conv + conv_w_ref[j:j + 1, :] * qkv_ext[lo:lo + ts, :]
    qkv_ext[0:CONV_HIST, :] = qkv_ext[ts:ts + CONV_HIST, :]
    qkv = _silu(conv)
    yield

    lane = lax.broadcasted_iota(jnp.int32, (ts, LANES), 1)
    beta_all = _sigmoid(bgp)
    g_all = -jnp.exp(a_log_ref[...]) * _softplus(bgp + dt_bias_ref[...])
    chunk_tril01 = jnp.concatenate(
        [jnp.concatenate([tril01 if q == p else jnp.zeros((PAIR, PAIR), BF16) for q in range(n_pairs)],
                         axis=1) for p in range(n_pairs)], axis=0)
    gc_n = _dot_exact_rhs(chunk_tril01, jnp.where(lane < heads, beta_all, g_all))
    beta_x = [jnp.broadcast_to(beta_all[:, hh:hh + 1], (ts, LANES)) for hh in range(heads)]
    gc_x = [jnp.broadcast_to(gc_n[:, heads + hh:heads + hh + 1], (ts, LANES)) for hh in range(heads)]
    gc_t = jnp.transpose(gc_n)
    yield

    for hh in range(heads):
        hs = slice(hh * hd, (hh + 1) * hd)
        q = qkv[:, hs]
        k = qkv[:, dn_width + hh * hd:dn_width + (hh + 1) * hd]
        v = qkv[:, 2 * dn_width + hh * hd:2 * dn_width + (hh + 1) * hd]
        q = q * lax.rsqrt(jnp.sum(q * q, axis=-1, keepdims=True) + EPS) * (hd ** -0.5)
        k = k * lax.rsqrt(jnp.sum(k * k, axis=-1, keepdims=True) + EPS)
        beta = beta_x[hh]
        gc = gc_x[hh]
        eg = jnp.exp(gc)
        kb = k * beta
        k16 = k.astype(BF16)
        kb16 = kb.astype(BF16)
        q16 = q.astype(BF16)
        st_rhs[slot, hh] = jnp.concatenate([v * beta, kb * eg], axis=1).astype(BF16)
        st_qd16[slot, hh] = (q * eg).astype(BF16)
        for p in range(n_pairs):
            rows = slice(p * PAIR, (p + 1) * PAIR)
            dm = gc[rows] - gc_t[heads + hh:heads + hh + 1, rows]
            gamma = jnp.exp(jnp.where(tril, dm, MASKED_LOG))
            st_l[slot, hh, p] = jnp.where(strict, _dot_nt(kb16[rows], k16[rows]) * gamma, 0.0)
            st_aqk[slot, hh, p] = (_dot_nt(q16[rows], k16[rows]) * gamma).astype(BF16)
            if p == n_pairs // 2 - 1:
                yield
        k_dec = []
        for c in range(n_chunks):
            rows = slice(c * CHUNK, (c + 1) * CHUNK)
            g_last = gc[(c + 1) * CHUNK - 1:(c + 1) * CHUNK, :]
            k_dec.append((k[rows] * jnp.exp(g_last - gc[rows])).astype(BF16))
            st_dec[slot, hh, c] = jnp.broadcast_to(jnp.exp(g_last), (8, hd))
        st_kdec[slot, hh] = jnp.concatenate(k_dec, axis=0)
        yield

    st_gate[slot] = _silu(_dot(hn, w_main_ref[:, o0 + 3 * dn_width:o0 + 4 * dn_width]))
    yield
    a_ext[POOL_HIST:POOL_HIST + ts, :] = a_in
    pos = t * ts + lax.broadcasted_iota(jnp.int32, (ts, LANES), 0)
    for gi, win in enumerate(POOL_WINDOWS):
        cols = slice(gi * LANES, (gi + 1) * LANES)
        cur = a_in[:, cols]
        s = cur
        for j in range(1, win):
            s = s + a_ext[POOL_HIST - j:POOL_HIST - j + ts, cols]
        cnt = jnp.minimum(pos + 1, win).astype(F32)
        pooled = s / cnt - cur
        mixed = _dot(pooled.astype(BF16), pool_w_ref[gi])
        st_ya[slot, :, cols] = (mixed * pool_scale_ref[:, cols]).astype(BF16)
        if gi % 2 == 1:
            yield
    a_ext[0:POOL_HIST, :] = a_ext[ts:ts + POOL_HIST, :]


def _ab_stage_state(h, slot, masks, out_norm_ref, w_out_ref, o_ref, state, stage, *, ts, heads):
    hd = LANES
    n_chunks = ts // CHUNK
    n_pairs = ts // PAIR
    per_pair = PAIR // CHUNK
    eye = masks[3]
    st_l, st_rhs, st_kdec, st_dec, st_aqk, st_qd16, st_ya, st_gate = stage
    blocks = [(hh, p) for hh in range(heads) for p in range(n_pairs)]

    lmats = [st_l[slot, hh, p] for hh, p in blocks]
    l16s = [m.astype(BF16) for m in lmats]
    invs = [eye - m for m in lmats]
    pws = [_dot(m, m) for m in l16s]
    yield
    for step in range(NEUMANN_STEPS):
        p16s = [m.astype(BF16) for m in pws]
        invs = [invs[i] + _dot(invs[i].astype(BF16), p16s[i]) for i in range(len(blocks))]
        if step + 1 < NEUMANN_STEPS:
            pws = [_dot(m, m) for m in p16s]
        yield

    us, w16s, ku_kws = {}, {}, {}

    def solve_pair(p):
        for hh in range(heads):
            i = hh * n_pairs + p
            sol = _dot(invs[i].astype(BF16), st_rhs[slot, hh, pl.ds(p * PAIR, PAIR), :])
            sol16 = sol.astype(BF16)
            us[i] = sol[:, :hd]
            w16s[i] = sol16[:, hd:]
            for lc in range(per_pair):
                c = p * per_pair + lc
                k_dec = st_kdec[slot, hh, pl.ds(c * CHUNK, CHUNK), :]
                ku_kws[(hh, c)] = _dot_tn(k_dec, sol16[lc * CHUNK:(lc + 1) * CHUNK])

    solve_pair(0)
    yield

    s_cur = [state[hh] for hh in range(heads)]
    vn_parts = [[] for _ in blocks]
    o_parts = [[] for _ in range(heads)]

    def outputs_of_chunk(c, ws_qs_all):
        p, lc = divmod(c, per_pair)
        local = slice(lc * CHUNK, (lc + 1) * CHUNK)
        for hh in range(heads):
            i = hh * n_pairs + p
            ws_qs = ws_qs_all[hh]
            v_new = us[i][local] - ws_qs[:CHUNK]
            vn_parts[i].append(v_new.astype(BF16))
            pad = [jnp.zeros((CHUNK, hd), BF16)] * (per_pair - lc - 1)
            vn_buf = jnp.concatenate(vn_parts[i] + pad, axis=0)
            aqk_rows = st_aqk[slot, hh, p, pl.ds(lc * CHUNK, CHUNK), :]
            o_parts[hh].append(ws_qs[CHUNK:] + _dot(aqk_rows, vn_buf))

    pending = None
    for c in range(n_chunks):
        p, lc = divmod(c, per_pair)
        local = slice(lc * CHUNK, (lc + 1) * CHUNK)
        s16s = [s_cur[hh].astype(BF16) for hh in range(heads)]
        for hh in range(heads):
            ku_kw = ku_kws[(hh, c)]
            s_cur[hh] = (s_cur[hh] * st_dec[slot, hh, c][0:1, :] + ku_kw[:, :hd]
                         - _dot(ku_kw[:, hd:].astype(BF16), s16s[hh]))
        ws_qs_all = []
        for hh in range(heads):
            i = hh * n_pairs + p
            wq = jnp.concatenate([w16s[i][local], st_qd16[slot, hh, pl.ds(c * CHUNK, CHUNK), :]], axis=0)
            ws_qs_all.append(_dot(wq, s16s[hh]))
        if pending is not None:
            outputs_of_chunk(*pending)
        pending = (c, ws_qs_all)
        if c + 1 < n_pairs:
            solve_pair(c + 1)
        yield
    outputs_of_chunk(*pending)
    for hh in range(heads):
        state[hh] = s_cur[hh]
    gate = st_gate[slot]
    o_heads = []
    for hh in range(heads):
        o = jnp.concatenate(o_parts[hh], axis=0)
        o_heads.append((_rmsnorm(o, out_norm_ref[...]) * gate[:, hh * hd:(hh + 1) * hd]).astype(BF16))
    y = jnp.concatenate([st_ya[slot]] + o_heads, axis=1)
    o_ref[0] = h + _dot(y, w_out_ref[...])
    yield


def _mixer_ab_body(h_front_ref, h_state_ref, g_ref, w_main_ref, w_bg_ref, pool_w_ref,
                   pool_scale_ref, conv_w_ref, a_log_ref, dt_bias_ref, out_norm_ref,
                   w_out_ref, o_ref, a_ext, qkv_ext, state, *stage,
                   ts, tiles_per_row, pool_width, dn_width, heads):
    g = pl.program_id(0)
    t_front = lax.rem(g, tiles_per_row)
    t_state = lax.rem(g + tiles_per_row - 1, tiles_per_row)
    slot_front = lax.rem(g, 2)
    slot_state = 1 - slot_front

    @pl.when(g == 0)
    def _():
        for ref in stage:
            ref[...] = jnp.zeros(ref.shape, ref.dtype)

    @pl.when(t_front == 0)
    def _():
        a_ext[0:POOL_HIST, :] = jnp.zeros((POOL_HIST, pool_width), F32)
        qkv_ext[0:CONV_HIST, :] = jnp.zeros((CONV_HIST, 3 * dn_width), F32)

    @pl.when((t_state == 0) | (g == 0))
    def _():
        state[...] = jnp.zeros(state.shape, F32)

    ri = lax.broadcasted_iota(jnp.int32, (PAIR, PAIR), 0)
    ci = lax.broadcasted_iota(jnp.int32, (PAIR, PAIR), 1)
    same = (ri ^ ci) < CHUNK
    tril = same & (ci <= ri)
    strict = same & (ci < ri)
    masks = (tril, strict, jnp.where(tril, 1.0, 0.0).astype(BF16), jnp.where(ri == ci, 1.0, 0.0))

    front = _ab_stage_front(
        h_front_ref[0], t_front, slot_front, masks, g_ref, w_main_ref, w_bg_ref, pool_w_ref,
        pool_scale_ref, conv_w_ref, a_log_ref, dt_bias_ref, a_ext, qkv_ext, stage,
        ts=ts, pool_width=pool_width, dn_width=dn_width, heads=heads)
    back = _ab_stage_state(h_state_ref[0], slot_state, masks, out_norm_ref, w_out_ref, o_ref,
                           state, stage, ts=ts, heads=heads)
    for who in AB_EMISSION_ORDER:
        next(back if who == "B" else front, None)
    for gen in (back, front):
        for _ in gen:
            pass


def _mixer_ab(h, norm_g, w_in, pool_w, pool_scale, conv_w, a_log, dt_bias, out_norm, w_out):
    b, s, d = h.shape
    heads = a_log.shape[0]
    pool_width = pool_scale.shape[0]
    dn_width = heads * LANES
    main = pool_width + 4 * dn_width
    ts = min(MIX_ROWS, s)
    tiles_per_row = s // ts
    n_tiles = b * tiles_per_row
    n_chunks = ts // CHUNK
    n_pairs = ts // PAIR
    w_main = w_in[:, :main].astype(BF16)
    w_bg = jnp.pad(w_in[:, main:main + 2 * heads], ((0, 0), (0, LANES - 2 * heads))).astype(BF16)
    gate_row = lambda p: jnp.pad(p, (heads, LANES - 2 * heads)).reshape(1, LANES)
    tiles = h.reshape(n_tiles, ts, d)
    front_spec = pl.BlockSpec((1, ts, d), lambda g: (jnp.minimum(g, n_tiles - 1), 0, 0))
    state_spec = pl.BlockSpec((1, ts, d), lambda g: (jnp.maximum(g - 1, 0), 0, 0))
    consts = [norm_g.reshape(1, d), w_main, w_bg, pool_w.astype(BF16),
              pool_scale.reshape(1, pool_width), conv_w, gate_row(a_log), gate_row(dt_bias),
              out_norm.reshape(1, LANES), w_out.astype(BF16)]
    stage_shapes = [
        pltpu.VMEM((2, heads, n_pairs, PAIR, PAIR), F32),
        pltpu.VMEM((2, heads, ts, 2 * LANES), BF16),
        pltpu.VMEM((2, heads, ts, LANES), BF16),
        pltpu.VMEM((2, heads, n_chunks, 8, LANES), F32),
        pltpu.VMEM((2, heads, n_pairs, PAIR, PAIR), BF16),
        pltpu.VMEM((2, heads, ts, LANES), BF16),
        pltpu.VMEM((2, ts, pool_width), BF16),
        pltpu.VMEM((2, ts, dn_width), F32),
    ]
    out = pl.pallas_call(
        functools.partial(_mixer_ab_body, ts=ts, tiles_per_row=tiles_per_row,
                          pool_width=pool_width, dn_width=dn_width, heads=heads),
        grid=(n_tiles + 1,),
        in_specs=[front_spec, state_spec] + [_const_spec(a.shape) for a in consts],
        out_specs=state_spec,
        out_shape=jax.ShapeDtypeStruct((n_tiles, ts, d), F32),
        scratch_shapes=[pltpu.VMEM((POOL_HIST + ts, pool_width), F32),
                        pltpu.VMEM((CONV_HIST + ts, 3 * dn_width), F32),
                        pltpu.VMEM((heads, LANES, LANES), F32)] + stage_shapes,
        compiler_params=pltpu.CompilerParams(
            dimension_semantics=("arbitrary",), vmem_limit_bytes=VMEM_LIMIT),
        name="mixer_ab",
    )(tiles, tiles, *consts)
    return out.reshape(b, s, d)


def _mixer_cd_body(h_ref, g_ref, w_in_ref, ln_g_ref, ln_b_ref, sgu_w_ref, sgu_bias_ref,
                   conv_w_ref, w_out_ref, o_ref, p_ext, *, ts, sgu_width, sc_width, heads):
    t = pl.program_id(1)
    n_blocks = ts // SGU_BLOCK

    @pl.when(t == 0)
    def _():
        p_ext[0:CONV_HIST, :] = jnp.zeros((CONV_HIST, sc_width), F32)

    h = h_ref[0]
    hn = _rmsnorm(h, g_ref[...]).astype(BF16)
    proj = _dot(hn, w_in_ref[...])
    uv = _gelu_tanh(proj[:, :2 * sgu_width])
    u = uv[:, :sgu_width]
    v = uv[:, sgu_width:]
    mu = jnp.mean(v, axis=-1, keepdims=True)
    vc = v - mu
    v = vc * lax.rsqrt(jnp.mean(vc * vc, axis=-1, keepdims=True) + EPS) * ln_g_ref[...] + ln_b_ref[...]
    v16 = v.astype(BF16)

    ri = lax.broadcasted_iota(jnp.int32, (SGU_BLOCK, SGU_BLOCK), 0)
    ci = lax.broadcasted_iota(jnp.int32, (SGU_BLOCK, SGU_BLOCK), 1)
    head_cols = []
    for hh in range(heads):
        hs = slice(hh * LANES, (hh + 1) * LANES)
        ws = jnp.where(ci <= ri, sgu_w_ref[hh], 0.0).astype(BF16)
        rhs = jnp.concatenate([v16[n * SGU_BLOCK:(n + 1) * SGU_BLOCK, hs] for n in range(n_blocks)],
                              axis=1)
        m = _dot(ws, rhs)
        head_cols.append(jnp.concatenate(
            [m[:, n * LANES:(n + 1) * LANES] for n in range(n_blocks)], axis=0))
    mixed = jnp.concatenate(head_cols, axis=1)
    bias = jnp.concatenate([sgu_bias_ref[...]] * n_blocks, axis=0)
    y_c = u * (mixed + bias)

    o0 = 2 * sgu_width
    xd = proj[:, o0:o0 + sc_width]
    bg = proj[:, o0 + sc_width:o0 + 2 * sc_width]
    cg = proj[:, o0 + 2 * sc_width:o0 + 3 * sc_width]
    p_ext[CONV_HIST:CONV_HIST + ts, :] = cg * xd
    conv = conv_w_ref[SC_CONV - 1:SC_CONV, :] * p_ext[CONV_HIST:CONV_HIST + ts, :]
    for j in range(SC_CONV - 1):
        lo = CONV_HIST - (SC_CONV - 1) + j
        conv = conv + conv_w_ref[j:j + 1, :] * p_ext[lo:lo + ts, :]
    p_ext[0:CONV_HIST, :] = p_ext[ts:ts + CONV_HIST, :]
    y_d = bg * conv

    y = jnp.concatenate([y_c, y_d], axis=1).astype(BF16)
    o_ref[0] = h + _dot(y, w_out_ref[...])


def _mixer_cd(h, norm_g, w_in, ln_g, ln_b, sgu_w, sgu_bias, conv_w, w_out):
    b, s, d = h.shape
    heads = sgu_w.shape[0]
    sgu_width = ln_g.shape[0]
    sc_width = conv_w.shape[1]
    ts = min(MIX_ROWS, s)
    bias_x = jnp.repeat(sgu_bias.T, LANES, axis=1)
    tok_spec = pl.BlockSpec((1, ts, d), lambda i, j: (i, j, 0))
    args = [h, norm_g.reshape(1, d), w_in.astype(BF16), ln_g.reshape(1, sgu_width),
            ln_b.reshape(1, sgu_width), sgu_w, bias_x, conv_w, w_out.astype(BF16)]
    in_specs = [tok_spec] + [_const_spec(a.shape) for a in args[1:]]
    return pl.pallas_call(
        functools.partial(_mixer_cd_body, ts=ts, sgu_width=sgu_width, sc_width=sc_width,
                          heads=heads),
        grid=(b, s // ts),
        in_specs=in_specs,
        out_specs=tok_spec,
        out_shape=jax.ShapeDtypeStruct((b, s, d), F32),
        scratch_shapes=[pltpu.VMEM((CONV_HIST + ts, sc_width), F32)],
        compiler_params=pltpu.CompilerParams(
            dimension_semantics=("arbitrary", "arbitrary"), vmem_limit_bytes=VMEM_LIMIT),
        name="mixer_cd",
    )(*args)


def kernel(x, ffn1_norm, ffn1_w_gate, ffn1_w_up, ffn1_w_down, mix_norm, ffn2_norm, ffn2_w_gate, ffn2_w_up, ffn2_w_down, ab_w_in, pool_w, pool_scale, dn_conv_w, dn_a_log, dn_dt_bias, dn_out_norm, ab_w_out, cd_w_in, sgu_norm_g, sgu_norm_b, sgu_w, sgu_bias, sc_conv_w, cd_w_out, final_norm):
    b, s, d = x.shape
    depth = ffn1_norm.shape[0]
    h = x.reshape(b * s, d)
    for layer in range(depth):
        h = _ffn(h, ffn1_norm[layer], ffn1_w_gate[layer], ffn1_w_up[layer], ffn1_w_down[layer])
        h3 = h.reshape(b, s, d)
        if layer % 2 == 0:
            e = layer // 2
            h3 = _mixer_ab(h3, mix_norm[layer], ab_w_in[e], pool_w[e], pool_scale[e], dn_conv_w[e],
                           dn_a_log[e], dn_dt_bias[e], dn_out_norm[e], ab_w_out[e])
        else:
            o = layer // 2
            h3 = _mixer_cd(h3, mix_norm[layer], cd_w_in[o], sgu_norm_g[o], sgu_norm_b[o], sgu_w[o],
                           sgu_bias[o], sc_conv_w[o], cd_w_out[o])
        h = h3.reshape(b * s, d)
        h = _ffn(h, ffn2_norm[layer], ffn2_w_gate[layer], ffn2_w_up[layer], ffn2_w_down[layer],
                 final_g=final_norm if layer == depth - 1 else None)
    return h.reshape(b, s, d)
```

```python
import functools

import jax
import jax.numpy as jnp
from jax import lax
from jax.experimental import pallas as pl
from jax.experimental.pallas import tpu as pltpu

F32 = jnp.float32
BF16 = jnp.bfloat16
EPS = 1e-6

LANES = 128
CHUNK = 64
POOL_WINDOWS = (2, 4, 8, 16)
POOL_HIST = 16
CONV_HIST = 8
DN_CONV = 4
SC_CONV = 3
SGU_BLOCK = 128
FFN_CHUNK = 256
NEUMANN_STEPS = 5

FFN_ROWS = 1024
MIX_ROWS = 512
CD_ROWS = 1024
AB_EMISSION_ORDER = "FFFFBBBBFBFBBFFBFFBFFBFFBFBFBFBBB"
PAIR = 2 * CHUNK
MASKED_LOG = -1e30
VMEM_LIMIT = 56 * 1024 * 1024


def _dot(a, b):
    return jnp.dot(a, b, preferred_element_type=F32)


def _dot_nt(a, b):
    return lax.dot_general(a, b, (((1,), (1,)), ((), ())), preferred_element_type=F32)


def _dot_tn(a, b):
    return lax.dot_general(a, b, (((0,), (0,)), ((), ())), preferred_element_type=F32)


def _split3(x):
    hi = x.astype(BF16)
    r = x - hi.astype(F32)
    mid = r.astype(BF16)
    lo = (r - mid.astype(F32)).astype(BF16)
    return hi, mid, lo


def _dot_exact_rhs(a01, x):
    hi, mid, lo = _split3(x)
    return _dot(a01, hi) + _dot(a01, mid) + _dot(a01, lo)


def _rmsnorm(x, g):
    return x * lax.rsqrt(jnp.mean(x * x, axis=-1, keepdims=True) + EPS) * g


def _sigmoid(x):
    return 1.0 / (1.0 + jnp.exp(-x))


def _silu(x):
    half = 0.5 * x
    return half + half * jnp.tanh(half)


def _gelu_tanh(x):
    return 0.5 * x * (1.0 + jnp.tanh(0.7978845608028654 * (x + 0.044715 * (x * x * x))))


def _softplus(x):
    return jnp.maximum(x, 0.0) + jnp.log1p(jnp.exp(-jnp.abs(x)))


def _ffn_body(*refs, n_chunks, final):
    if final:
        h_ref, g_ref, wg_ref, wu_ref, wd_ref, fg_ref, o_ref = refs
    else:
        h_ref, g_ref, wg_ref, wu_ref, wd_ref, o_ref = refs
    h = h_ref[...]
    hn = _rmsnorm(h, g_ref[...]).astype(BF16)
    acc = None
    for c in range(n_chunks):
        cols = slice(c * FFN_CHUNK, (c + 1) * FFN_CHUNK)
        gate = _dot(hn, wg_ref[:, cols])
        up = _dot(hn, wu_ref[:, cols])
        act = (_silu(gate) * up).astype(BF16)
        part = _dot(act, wd_ref[cols, :])
        acc = part if acc is None else acc + part
    out = h + 0.5 * acc
    if final:
        out = _rmsnorm(out, fg_ref[...])
    o_ref[...] = out


def _const_spec(shape):
    zeros = (0,) * len(shape)
    return pl.BlockSpec(shape, lambda *_: zeros, pipeline_mode=pl.Buffered(1))


def _layer_spec(stacked, layer):
    tail = (0,) * (stacked.ndim - 1)
    return pl.BlockSpec((None,) + stacked.shape[1:], lambda *_: (layer,) + tail,
                        pipeline_mode=pl.Buffered(1))


def _ffn(h, norm_g, w_gate, w_up, w_down, layer, final_g=None):
    n, d = h.shape
    f = w_gate.shape[2]
    n_chunks = f // FFN_CHUNK
    rows = min(FFN_ROWS, n)
    row_spec = pl.BlockSpec((rows, d), lambda i: (i, 0))
    args = [h, norm_g.reshape(1, d), w_gate, w_up, w_down]
    in_specs = [row_spec, _const_spec((1, d)), _layer_spec(w_gate, layer), _layer_spec(w_up, layer),
                _layer_spec(w_down, layer)]
    if final_g is not None:
        args.append(final_g.reshape(1, d))
        in_specs.append(_const_spec((1, d)))
    return pl.pallas_call(
        functools.partial(_ffn_body, n_chunks=n_chunks, final=final_g is not None),
        grid=(n // rows,),
        in_specs=in_specs,
        out_specs=row_spec,
        out_shape=jax.ShapeDtypeStruct((n, d), F32),
        compiler_params=pltpu.CompilerParams(
            dimension_semantics=("parallel",), vmem_limit_bytes=VMEM_LIMIT),
        name="ffn_final" if final_g is not None else "ffn",
    )(*args)


def _ab_stage_front(h, t, slot, masks, g_ref, w_main_ref, w_bg_ref, pool_w_ref, pool_scale_ref,
                    conv_w_ref, a_log_ref, dt_bias_ref, a_ext, qkv_ext, stage,
                    *, ts, pool_width, dn_width, heads):
    hd = LANES
    n_chunks = ts // CHUNK
    n_pairs = ts // PAIR
    tril, strict, tril01, _ = masks
    st_l, st_rhs, st_kdec, st_dec, st_aqk, st_qd16, st_ya, st_gate = stage
    o0 = pool_width
    hn = _rmsnorm(h, g_ref[...]).astype(BF16)
    bgp = _dot(hn, w_bg_ref[...])
    a_in = _dot(hn, w_main_ref[:, 0:pool_width])
    yield
    for part in range(3):
        cols = slice(o0 + part * dn_width, o0 + (part + 1) * dn_width)
        qkv_ext[CONV_HIST:CONV_HIST + ts, part * dn_width:(part + 1) * dn_width] = _dot(hn, w_main_ref[:, cols])
        yield

    conv = conv_w_ref[DN_CONV - 1:DN_CONV, :] * qkv_ext[CONV_HIST:CONV_HIST + ts, :]
    for j in range(DN_CONV - 1):
        lo = CONV_HIST - (DN_CONV - 1) + j
        conv = conv + conv_w_ref[j:j + 1, :] * qkv_ext[lo:lo + ts, :]
    qkv_ext[0:CONV_HIST, :] = qkv_ext[ts:ts + CONV_HIST, :]
    qkv = _silu(conv)
    yield

    lane = lax.broadcasted_iota(jnp.int32, (ts, LANES), 1)
    beta_all = _sigmoid(bgp)
    g_all = -jnp.exp(a_log_ref[...]) * _softplus(bgp + dt_bias_ref[...])
    chunk_tril01 = jnp.concatenate(
        [jnp.concatenate([tril01 if q == p else jnp.zeros((PAIR, PAIR), BF16) for q in range(n_pairs)],
                         axis=1) for p in range(n_pairs)], axis=0)
    gc_n = _dot_exact_rhs(chunk_tril01, jnp.where(lane < heads, beta_all, g_all))
    beta_x = [jnp.broadcast_to(beta_all[:, hh:hh + 1], (ts, LANES)) for hh in range(heads)]
    gc_x = [jnp.broadcast_to(gc_n[:, heads + hh:heads + hh + 1], (ts, LANES)) for hh in range(heads)]
    gc_t = jnp.transpose(gc_n)
    yield

    for hh in range(heads):
        hs = slice(hh * hd, (hh + 1) * hd)
        q = qkv[:, hs]
        k = qkv[:, dn_width + hh * hd:dn_width + (hh + 1) * hd]
        v = qkv[:, 2 * dn_width + hh * hd:2 * dn_width + (hh + 1) * hd]
        q = q * lax.rsqrt(jnp.sum(q * q, axis=-1, keepdims=True) + EPS) * (hd ** -0.5)
        k = k * lax.rsqrt(jnp.sum(k * k, axis=-1, keepdims=True) + EPS)
        beta = beta_x[hh]
        gc = gc_x[hh]
        eg = jnp.exp(gc)
        kb = k * beta
        k16 = k.astype(BF16)
        kb16 = kb.astype(BF16)
        q16 = q.astype(BF16)
        st_rhs[slot, hh] = jnp.concatenate([v * beta, kb * eg], axis=1).astype(BF16)
        st_qd16[slot, hh] = (q * eg).astype(BF16)
        for p in range(n_pairs):
            rows = slice(p * PAIR, (p + 1) * PAIR)
            dm = gc[rows] - gc_t[heads + hh:heads + hh + 1, rows]
            gamma = jnp.exp(jnp.where(tril, dm, MASKED_LOG))
            st_l[slot, hh, p] = jnp.where(strict, _dot_nt(kb16[rows], k16[rows]) * gamma, 0.0)
            st_aqk[slot, hh, p] = (_dot_nt(q16[rows], k16[rows]) * gamma).astype(BF16)
            if p == n_pairs // 2 - 1:
                yield
        k_dec = []
        for c in range(n_chunks):
            rows = slice(c * CHUNK, (c + 1) * CHUNK)
            g_last = gc[(c + 1) * CHUNK - 1:(c + 1) * CHUNK, :]
            k_dec.append((k[rows] * jnp.exp(g_last - gc[rows])).astype(BF16))
            st_dec[slot, hh, c] = jnp.broadcast_to(jnp.exp(g_last), (8, hd))
        st_kdec[slot, hh] = jnp.concatenate(k_dec, axis=0)
        yield

    st_gate[slot] = _silu(_dot(hn, w_main_ref[:, o0 + 3 * dn_width:o0 + 4 * dn_width]))
    yield
    a_ext[POOL_HIST:POOL_HIST + ts, :] = a_in
    pos = t * ts + lax.broadcasted_iota(jnp.int32, (ts, LANES), 0)
    for gi, win in enumerate(POOL_WINDOWS):
        cols = slice(gi * LANES, (gi + 1) * LANES)
        cur = a_in[:, cols]
        s = cur
        for j in range(1, win):
            s = s + a_ext[POOL_HIST - j:POOL_HIST - j + ts, cols]
        cnt = jnp.minimum(pos + 1, win).astype(F32)
        pooled = s / cnt - cur
        mixed = _dot(pooled.astype(BF16), pool_w_ref[gi])
        st_ya[slot, :, cols] = (mixed * pool_scale_ref[:, cols]).astype(BF16)
        if gi % 2 == 1:
            yield
    a_ext[0:POOL_HIST, :] = a_ext[ts:ts + POOL_HIST, :]


def _ab_stage_state(h, slot, masks, out_norm_ref, w_out_ref, o_ref, state, stage, *, ts, heads):
    hd = LANES
    n_chunks = ts // CHUNK
    n_pairs = ts // PAIR
    per_pair = PAIR // CHUNK
    eye = masks[3]
    st_l, st_rhs, st_kdec, st_dec, st_aqk, st_qd16, st_ya, st_gate = stage
    blocks = [(hh, p) for hh in range(heads) for p in range(n_pairs)]

    lmats = [st_l[slot, hh, p] for hh, p in blocks]
    l16s = [m.astype(BF16) for m in lmats]
    invs = [eye - m for m in lmats]
    pws = [_dot(m, m) for m in l16s]
    yield
    for step in range(NEUMANN_STEPS):
        p16s = [m.astype(BF16) for m in pws]
        invs = [invs[i] + _dot(invs[i].astype(BF16), p16s[i]) for i in range(len(blocks))]
        if step + 1 < NEUMANN_STEPS:
            pws = [_dot(m, m) for m in p16s]
        yield

    us, w16s, ku_kws = {}, {}, {}

    def solve_pair(p):
        for hh in range(heads):
            i = hh * n_pairs + p
            sol = _dot(invs[i].astype(BF16), st_rhs[slot, hh, pl.ds(p * PAIR, PAIR), :])
            sol16 = sol.astype(BF16)
            us[i] = sol[:, :hd]
            w16s[i] = sol16[:, hd:]
            for lc in range(per_pair):
                c = p * per_pair + lc
                k_dec = st_kdec[slot, hh, pl.ds(c * CHUNK, CHUNK), :]
                ku_kws[(hh, c)] = _dot_tn(k_dec, sol16[lc * CHUNK:(lc + 1) * CHUNK])

    solve_pair(0)
    yield

    s_cur = [state[hh] for hh in range(heads)]
    vn_parts = [[] for _ in blocks]
    o_parts = [[] for _ in range(heads)]

    def outputs_of_chunk(c, ws_qs_all):
        p, lc = divmod(c, per_pair)
        local = slice(lc * CHUNK, (lc + 1) * CHUNK)
        for hh in range(heads):
            i = hh * n_pairs + p
            ws_qs = ws_qs_all[hh]
            v_new = us[i][local] - ws_qs[:CHUNK]
            vn_parts[i].append(v_new.astype(BF16))
            pad = [jnp.zeros((CHUNK, hd), BF16)] * (per_pair - lc - 1)
            vn_buf = jnp.concatenate(vn_parts[i] + pad, axis=0)
            aqk_rows = st_aqk[slot, hh, p, pl.ds(lc * CHUNK, CHUNK), :]
            o_parts[hh].append(ws_qs[CHUNK:] + _dot(aqk_rows, vn_buf))

    pending = None
    for c in range(n_chunks):
        p, lc = divmod(c, per_pair)
        local = slice(lc * CHUNK, (lc + 1) * CHUNK)
        s16s = [s_cur[hh].astype(BF16) for hh in range(heads)]
        for hh in range(heads):
            ku_kw = ku_kws[(hh, c)]
            s_cur[hh] = (s_cur[hh] * st_dec[slot, hh, c][0:1, :] + ku_kw[:, :hd]
                         - _dot(ku_kw[:, hd:].astype(BF16), s16s[hh]))
        ws_qs_all = []
        for hh in range(heads):
            i = hh * n_pairs + p
            wq = jnp.concatenate([w16s[i][local], st_qd16[slot, hh, pl.ds(c * CHUNK, CHUNK), :]], axis=0)
            ws_qs_all.append(_dot(wq, s16s[hh]))
        if pending is not None:
            outputs_of_chunk(*pending)
        pending = (c, ws_qs_all)
        if c + 1 < n_pairs:
            solve_pair(c + 1)
        yield
    outputs_of_chunk(*pending)
    for hh in range(heads):
        state[hh] = s_cur[hh]
    gate = st_gate[slot]
    o_heads = []
    for hh in range(heads):
        o = jnp.concatenate(o_parts[hh], axis=0)
        o_heads.append((_rmsnorm(o, out_norm_ref[...]) * gate[:, hh * hd:(hh + 1) * hd]).astype(BF16))
    y = jnp.concatenate([st_ya[slot]] + o_heads, axis=1)
    o_ref[0] = h + _dot(y, w_out_ref[...])
    yield


def _mixer_ab_body(h_front_ref, h_state_ref, g_ref, w_main_ref, w_bg_ref, pool_w_ref,
                   pool_scale_ref, conv_w_ref, a_log_ref, dt_bias_ref, out_norm_ref,
                   w_out_ref, o_ref, a_ext, qkv_ext, state, *stage,
                   ts, tiles_per_row, pool_width, dn_width, heads):
    g = pl.program_id(0)
    t_front = lax.rem(g, tiles_per_row)
    t_state = lax.rem(g + tiles_per_row - 1, tiles_per_row)
    slot_front = lax.rem(g, 2)
    slot_state = 1 - slot_front

    @pl.when(g == 0)
    def _():
        for ref in stage:
            ref[...] = jnp.zeros(ref.shape, ref.dtype)

    @pl.when(t_front == 0)
    def _():
        a_ext[0:POOL_HIST, :] = jnp.zeros((POOL_HIST, pool_width), F32)
        qkv_ext[0:CONV_HIST, :] = jnp.zeros((CONV_HIST, 3 * dn_width), F32)

    @pl.when((t_state == 0) | (g == 0))
    def _():
        state[...] = jnp.zeros(state.shape, F32)

    ri = lax.broadcasted_iota(jnp.int32, (PAIR, PAIR), 0)
    ci = lax.broadcasted_iota(jnp.int32, (PAIR, PAIR), 1)
    same = (ri ^ ci) < CHUNK
    tril = same & (ci <= ri)
    strict = same & (ci < ri)
    masks = (tril, strict, jnp.where(tril, 1.0, 0.0).astype(BF16), jnp.where(ri == ci, 1.0, 0.0))

    front = _ab_stage_front(
        h_front_ref[0], t_front, slot_front, masks, g_ref, w_main_ref, w_bg_ref, pool_w_ref,
        pool_scale_ref, conv_w_ref, a_log_ref, dt_bias_ref, a_ext, qkv_ext, stage,
        ts=ts, pool_width=pool_width, dn_width=dn_width, heads=heads)
    back = _ab_stage_state(h_state_ref[0], slot_state, masks, out_norm_ref, w_out_ref, o_ref,
                           state, stage, ts=ts, heads=heads)
    for who in AB_EMISSION_ORDER:
        next(back if who == "B" else front, None)
    for gen in (back, front):
        for _ in gen:
            pass


def _mixer_ab(h, norm_g, w_main, w_bg, pool_w, pool_scale, conv_w, a_log, dt_bias, out_norm, w_out,
              layer):
    b, s, d = h.shape
    heads = a_log.shape[0]
    pool_width = pool_scale.shape[0]
    dn_width = heads * LANES
    ts = min(MIX_ROWS, s)
    tiles_per_row = s // ts
    n_tiles = b * tiles_per_row
    n_chunks = ts // CHUNK
    n_pairs = ts // PAIR
    gate_row = lambda p: jnp.pad(p, (heads, LANES - 2 * heads)).reshape(1, LANES)
    tiles = h.reshape(n_tiles, ts, d)
    front_spec = pl.BlockSpec((1, ts, d), lambda g: (jnp.minimum(g, n_tiles - 1), 0, 0))
    state_spec = pl.BlockSpec((1, ts, d), lambda g: (jnp.maximum(g - 1, 0), 0, 0))
    consts = [norm_g.reshape(1, d), w_main, w_bg, pool_w,
              pool_scale.reshape(1, pool_width), conv_w, gate_row(a_log), gate_row(dt_bias),
              out_norm.reshape(1, LANES), w_out]
    stacked = (1, 2, 3, 9)
    const_specs = [_layer_spec(a, layer) if i in stacked else _const_spec(a.shape)
                   for i, a in enumerate(consts)]
    stage_shapes = [
        pltpu.VMEM((2, heads, n_pairs, PAIR, PAIR), F32),
        pltpu.VMEM((2, heads, ts, 2 * LANES), BF16),
        pltpu.VMEM((2, heads, ts, LANES), BF16),
        pltpu.VMEM((2, heads, n_chunks, 8, LANES), F32),
        pltpu.VMEM((2, heads, n_pairs, PAIR, PAIR), BF16),
        pltpu.VMEM((2, heads, ts, LANES), BF16),
        pltpu.VMEM((2, ts, pool_width), BF16),
        pltpu.VMEM((2, ts, dn_width), F32),
    ]
    out = pl.pallas_call(
        functools.partial(_mixer_ab_body, ts=ts, tiles_per_row=tiles_per_row,
                          pool_width=pool_width, dn_width=dn_width, heads=heads),
        grid=(n_tiles + 1,),
        in_specs=[front_spec, state_spec] + const_specs,
        out_specs=state_spec,
        out_shape=jax.ShapeDtypeStruct((n_tiles, ts, d), F32),
        scratch_shapes=[pltpu.VMEM((POOL_HIST + ts, pool_width), F32),
                        pltpu.VMEM((CONV_HIST + ts, 3 * dn_width), F32),
                        pltpu.VMEM((heads, LANES, LANES), F32)] + stage_shapes,
        compiler_params=pltpu.CompilerParams(
            dimension_semantics=("arbitrary",), vmem_limit_bytes=VMEM_LIMIT),
        name="mixer_ab",
    )(tiles, tiles, *consts)
    return out.reshape(b, s, d)


def _mixer_cd_body(h_ref, g_ref, w_in_ref, ln_g_ref, ln_b_ref, sgu_w_ref, sgu_bias_ref,
                   conv_w_ref, w_out_ref, o_ref, p_ext, *, ts, sgu_width, sc_width, heads):
    t = pl.program_id(1)
    n_blocks = ts // SGU_BLOCK

    @pl.when(t == 0)
    def _():
        p_ext[0:CONV_HIST, :] = jnp.zeros((CONV_HIST, sc_width), F32)

    h = h_ref[0]
    hn = _rmsnorm(h, g_ref[...]).astype(BF16)
    proj = _dot(hn, w_in_ref[...])
    uv = _gelu_tanh(proj[:, :2 * sgu_width])
    u = uv[:, :sgu_width]
    v = uv[:, sgu_width:]
    mu = jnp.mean(v, axis=-1, keepdims=True)
    vc = v - mu
    v = vc * lax.rsqrt(jnp.mean(vc * vc, axis=-1, keepdims=True) + EPS) * ln_g_ref[...] + ln_b_ref[...]
    v16 = v.astype(BF16)

    ri = lax.broadcasted_iota(jnp.int32, (SGU_BLOCK, SGU_BLOCK), 0)
    ci = lax.broadcasted_iota(jnp.int32, (SGU_BLOCK, SGU_BLOCK), 1)
    head_cols = []
    for hh in range(heads):
        hs = slice(hh * LANES, (hh + 1) * LANES)
        ws = jnp.where(ci <= ri, sgu_w_ref[hh], 0.0).astype(BF16)
        rhs = jnp.concatenate([v16[n * SGU_BLOCK:(n + 1) * SGU_BLOCK, hs] for n in range(n_blocks)],
                              axis=1)
        m = _dot(ws, rhs)
        head_cols.append(jnp.concatenate(
            [m[:, n * LANES:(n + 1) * LANES] for n in range(n_blocks)], axis=0))
    mixed = jnp.concatenate(head_cols, axis=1)
    bias = jnp.concatenate([sgu_bias_ref[...]] * n_blocks, axis=0)
    y_c = u * (mixed + bias)

    o0 = 2 * sgu_width
    xd = proj[:, o0:o0 + sc_width]
    bg = proj[:, o0 + sc_width:o0 + 2 * sc_width]
    cg = proj[:, o0 + 2 * sc_width:o0 + 3 * sc_width]
    p_ext[CONV_HIST:CONV_HIST + ts, :] = cg * xd
    conv = conv_w_ref[SC_CONV - 1:SC_CONV, :] * p_ext[CONV_HIST:CONV_HIST + ts, :]
    for j in range(SC_CONV - 1):
        lo = CONV_HIST - (SC_CONV - 1) + j
        conv = conv + conv_w_ref[j:j + 1, :] * p_ext[lo:lo + ts, :]
    p_ext[0:CONV_HIST, :] = p_ext[ts:ts + CONV_HIST, :]
    y_d = bg * conv

    y = jnp.concatenate([y_c, y_d], axis=1).astype(BF16)
    o_ref[0] = h + _dot(y, w_out_ref[...])


def _mixer_cd(h, norm_g, w_in, ln_g, ln_b, sgu_w, sgu_bias, conv_w, w_out, layer):
    b, s, d = h.shape
    heads = sgu_w.shape[0]
    sgu_width = ln_g.shape[0]
    sc_width = conv_w.shape[1]
    ts = min(CD_ROWS, s)
    bias_x = jnp.repeat(sgu_bias.T, LANES, axis=1)
    tok_spec = pl.BlockSpec((1, ts, d), lambda i, j: (i, j, 0))
    args = [h, norm_g.reshape(1, d), w_in, ln_g.reshape(1, sgu_width),
            ln_b.reshape(1, sgu_width), sgu_w, bias_x, conv_w, w_out]
    stacked = (2, 8)
    in_specs = [tok_spec] + [_layer_spec(a, layer) if i in stacked else _const_spec(a.shape)
                             for i, a in enumerate(args) if i > 0]
    return pl.pallas_call(
        functools.partial(_mixer_cd_body, ts=ts, sgu_width=sgu_width, sc_width=sc_width,
                          heads=heads),
        grid=(b, s // ts),
        in_specs=in_specs,
        out_specs=tok_spec,
        out_shape=jax.ShapeDtypeStruct((b, s, d), F32),
        scratch_shapes=[pltpu.VMEM((CONV_HIST + ts, sc_width), F32)],
        compiler_params=pltpu.CompilerParams(
            dimension_semantics=("arbitrary", "arbitrary"), vmem_limit_bytes=VMEM_LIMIT),
        name="mixer_cd",
    )(*args)


def kernel(x, ffn1_norm, ffn1_w_gate, ffn1_w_up, ffn1_w_down, mix_norm, ffn2_norm, ffn2_w_gate, ffn2_w_up, ffn2_w_down, ab_w_in, pool_w, pool_scale, dn_conv_w, dn_a_log, dn_dt_bias, dn_out_norm, ab_w_out, cd_w_in, sgu_norm_g, sgu_norm_b, sgu_w, sgu_bias, sc_conv_w, cd_w_out, final_norm):
    b, s, d = x.shape
    depth = ffn1_norm.shape[0]
    h = x.reshape(b * s, d)
    ffn1 = [w.astype(BF16) for w in (ffn1_w_gate, ffn1_w_up, ffn1_w_down)]
    ffn2 = [w.astype(BF16) for w in (ffn2_w_gate, ffn2_w_up, ffn2_w_down)]
    ab_heads = dn_a_log.shape[1]
    ab_main = pool_scale.shape[1] + 4 * ab_heads * LANES
    ab_w_main = ab_w_in[:, :, :ab_main].astype(BF16)
    ab_w_bg = jnp.pad(ab_w_in[:, :, ab_main:ab_main + 2 * ab_heads],
                      ((0, 0), (0, 0), (0, LANES - 2 * ab_heads))).astype(BF16)
    ab_pool_w, ab_out = pool_w.astype(BF16), ab_w_out.astype(BF16)
    cd_in, cd_out = cd_w_in.astype(BF16), cd_w_out.astype(BF16)
    for layer in range(depth):
        h = _ffn(h, ffn1_norm[layer], *ffn1, layer)
        h3 = h.reshape(b, s, d)
        if layer % 2 == 0:
            e = layer // 2
            h3 = _mixer_ab(h3, mix_norm[layer], ab_w_main, ab_w_bg, ab_pool_w, pool_scale[e],
                           dn_conv_w[e], dn_a_log[e], dn_dt_bias[e], dn_out_norm[e], ab_out, e)
        else:
            o = layer // 2
            h3 = _mixer_cd(h3, mix_norm[layer], cd_in, sgu_norm_g[o], sgu_norm_b[o], sgu_w[o],
                           sgu_bias[o], sc_conv_w[o], cd_out, o)
        h = h3.reshape(b * s, d)
        h = _ffn(h, ffn2_norm[layer], *ffn2, layer,
                 final_g=final_norm if layer == depth - 1 else None)
    return h.reshape(b, s, d)
```

```python
import functools

import jax
import jax.numpy as jnp
from jax import lax
from jax.experimental import pallas as pl
from jax.experimental.pallas import tpu as pltpu

F32 = jnp.float32
BF16 = jnp.bfloat16
EPS = 1e-6

LANES = 128
SUBLANES = 8
CHUNK = 64
POOL_WINDOWS = (2, 4, 8, 16)
POOL_HIST = 16
CONV_HIST = 8
DN_CONV = 4
SC_CONV = 3
SGU_BLOCK = 128
FFN_CHUNK = 256
NEUMANN_STEPS = 5

FFN_ROWS = 1024
MIX_ROWS = 512
CD_ROWS = 1024
AB_EMISSION_ORDER = "FFFFBBBBFBFBBFFBFFBFFBFFBFBFBFBBB"
PAIR = 2 * CHUNK
MASKED_LOG = -1e30
VMEM_LIMIT = 56 * 1024 * 1024


def _dot(a, b):
    return jnp.dot(a, b, preferred_element_type=F32)


def _dot_nt(a, b):
    return lax.dot_general(a, b, (((1,), (1,)), ((), ())), preferred_element_type=F32)


def _dot_tn(a, b):
    return lax.dot_general(a, b, (((0,), (0,)), ((), ())), preferred_element_type=F32)


def _split3(x):
    hi = x.astype(BF16)
    r = x - hi.astype(F32)
    mid = r.astype(BF16)
    lo = (r - mid.astype(F32)).astype(BF16)
    return hi, mid, lo


def _dot_exact_rhs(a01, x):
    hi, mid, lo = _split3(x)
    return _dot(a01, hi) + _dot(a01, mid) + _dot(a01, lo)


def _rmsnorm(x, g):
    return x * lax.rsqrt(jnp.mean(x * x, axis=-1, keepdims=True) + EPS) * g


def _sigmoid(x):
    return 1.0 / (1.0 + jnp.exp(-x))


def _silu(x):
    half = 0.5 * x
    return half + half * jnp.tanh(half)


def _gelu_tanh(x):
    return 0.5 * x * (1.0 + jnp.tanh(0.7978845608028654 * (x + 0.044715 * (x * x * x))))


def _softplus(x):
    return jnp.maximum(x, 0.0) + jnp.log1p(jnp.exp(-jnp.abs(x)))


def _ffn_body(*refs, n_chunks, final):
    if final:
        h_ref, g_ref, wg_ref, wu_ref, wd_ref, fg_ref, o_ref = refs
    else:
        h_ref, g_ref, wg_ref, wu_ref, wd_ref, o_ref = refs
    h = h_ref[...]
    hn = _rmsnorm(h, g_ref[...]).astype(BF16)
    acc = None
    for c in range(n_chunks):
        cols = slice(c * FFN_CHUNK, (c + 1) * FFN_CHUNK)
        gate = _dot(hn, wg_ref[:, cols])
        up = _dot(hn, wu_ref[:, cols])
        act = (_silu(gate) * up).astype(BF16)
        part = _dot(act, wd_ref[cols, :])
        acc = part if acc is None else acc + part
    out = h + 0.5 * acc
    if final:
        out = _rmsnorm(out, fg_ref[...])
    o_ref[...] = out


def _const_spec(shape):
    zeros = (0,) * len(shape)
    return pl.BlockSpec(shape, lambda *_: zeros, pipeline_mode=pl.Buffered(1))


def _layer_spec(stacked, layer):
    tail = (0,) * (stacked.ndim - 1)
    return pl.BlockSpec((None,) + stacked.shape[1:], lambda *_: (layer,) + tail,
                        pipeline_mode=pl.Buffered(1))


def _ffn(h, norm_g, w_gate, w_up, w_down, layer, final_g=None):
    n, d = h.shape
    f = w_gate.shape[2]
    n_chunks = f // FFN_CHUNK
    rows = min(FFN_ROWS, n)
    row_spec = pl.BlockSpec((rows, d), lambda i: (i, 0))
    args = [h, norm_g.reshape(1, d), w_gate, w_up, w_down]
    in_specs = [row_spec, _const_spec((1, d)), _layer_spec(w_gate, layer), _layer_spec(w_up, layer),
                _layer_spec(w_down, layer)]
    if final_g is not None:
        args.append(final_g.reshape(1, d))
        in_specs.append(_const_spec((1, d)))
    return pl.pallas_call(
        functools.partial(_ffn_body, n_chunks=n_chunks, final=final_g is not None),
        grid=(n // rows,),
        in_specs=in_specs,
        out_specs=row_spec,
        out_shape=jax.ShapeDtypeStruct((n, d), F32),
        compiler_params=pltpu.CompilerParams(
            dimension_semantics=("parallel",), vmem_limit_bytes=VMEM_LIMIT),
        name="ffn_final" if final_g is not None else "ffn",
    )(*args)


def _ab_stage_front(h, t, slot, masks, g_ref, w_main_ref, w_bg_ref, pool_w_ref, pool_scale_ref,
                    conv_w_ref, a_log_ref, dt_bias_ref, a_ext, qkv_ext, stage,
                    *, ts, pool_width, dn_width, heads):
    hd = LANES
    n_chunks = ts // CHUNK
    n_pairs = ts // PAIR
    tril, strict, tril01, _ = masks
    st_l, st_rhs, st_kdec, st_dec, st_aqk, st_qd16, st_ya, st_gate = stage
    o0 = pool_width
    hn = _rmsnorm(h, g_ref[...]).astype(BF16)
    bgp = _dot(hn, w_bg_ref[...])
    a_in = _dot(hn, w_main_ref[:, 0:pool_width])
    yield
    for part in range(3):
        cols = slice(o0 + part * dn_width, o0 + (part + 1) * dn_width)
        qkv_ext[CONV_HIST:CONV_HIST + ts, part * dn_width:(part + 1) * dn_width] = _dot(hn, w_main_ref[:, cols])
        yield

    conv = conv_w_ref[DN_CONV - 1:DN_CONV, :] * qkv_ext[CONV_HIST:CONV_HIST + ts, :]
    for j in range(DN_CONV - 1):
        lo = CONV_HIST - (DN_CONV - 1) + j
        conv = conv + conv_w_ref[j:j + 1, :] * qkv_ext[lo:lo + ts, :]
    qkv_ext[0:CONV_HIST, :] = qkv_ext[ts:ts + CONV_HIST, :]
    qkv = _silu(conv)
    yield

    lane = lax.broadcasted_iota(jnp.int32, (ts, LANES), 1)
    beta_all = _sigmoid(bgp)
    g_all = -jnp.exp(a_log_ref[...]) * _softplus(bgp + dt_bias_ref[...])
    chunk_tril01 = jnp.concatenate(
        [jnp.concatenate([tril01 if q == p else jnp.zeros((PAIR, PAIR), BF16) for q in range(n_pairs)],
                         axis=1) for p in range(n_pairs)], axis=0)
    gc_n = _dot_exact_rhs(chunk_tril01, jnp.where(lane < heads, beta_all, g_all))
    beta_x = [jnp.broadcast_to(beta_all[:, hh:hh + 1], (ts, LANES)) for hh in range(heads)]
    gc_x = [jnp.broadcast_to(gc_n[:, heads + hh:heads + hh + 1], (ts, LANES)) for hh in range(heads)]
    gc_t = jnp.transpose(gc_n)
    yield

    for hh in range(heads):
        hs = slice(hh * hd, (hh + 1) * hd)
        q = qkv[:, hs]
        k = qkv[:, dn_width + hh * hd:dn_width + (hh + 1) * hd]
        v = qkv[:, 2 * dn_width + hh * hd:2 * dn_width + (hh + 1) * hd]
        q = q * lax.rsqrt(jnp.sum(q * q, axis=-1, keepdims=True) + EPS) * (hd ** -0.5)
        k = k * lax.rsqrt(jnp.sum(k * k, axis=-1, keepdims=True) + EPS)
        beta = beta_x[hh]
        gc = gc_x[hh]
        eg = jnp.exp(gc)
        kb = k * beta
        k16 = k.astype(BF16)
        kb16 = kb.astype(BF16)
        q16 = q.astype(BF16)
        st_rhs[slot, hh] = jnp.concatenate([v * beta, kb * eg], axis=1).astype(BF16)
        st_qd16[slot, hh] = (q * eg).astype(BF16)
        for p in range(n_pairs):
            rows = slice(p * PAIR, (p + 1) * PAIR)
            dm = gc[rows] - gc_t[heads + hh:heads + hh + 1, rows]
            gamma = jnp.exp(jnp.where(tril, dm, MASKED_LOG))
            st_l[slot, hh, p] = jnp.where(strict, _dot_nt(kb16[rows], k16[rows]) * gamma, 0.0)
            st_aqk[slot, hh, p] = (_dot_nt(q16[rows], k16[rows]) * gamma).astype(BF16)
            if p == n_pairs // 2 - 1:
                yield
        k_dec = []
        for c in range(n_chunks):
            rows = slice(c * CHUNK, (c + 1) * CHUNK)
            g_last = gc[(c + 1) * CHUNK - 1:(c + 1) * CHUNK, :]
            k_dec.append((k[rows] * jnp.exp(g_last - gc[rows])).astype(BF16))
            st_dec[slot, hh, c] = jnp.broadcast_to(jnp.exp(g_last), (SUBLANES, hd))
        st_kdec[slot, hh] = jnp.concatenate(k_dec, axis=0)
        yield

    st_gate[slot] = _silu(_dot(hn, w_main_ref[:, o0 + 3 * dn_width:o0 + 4 * dn_width]))
    yield
    a_ext[POOL_HIST:POOL_HIST + ts, :] = a_in
    pos = t * ts + lax.broadcasted_iota(jnp.int32, (ts, LANES), 0)
    for gi, win in enumerate(POOL_WINDOWS):
        cols = slice(gi * LANES, (gi + 1) * LANES)
        cur = a_in[:, cols]
        s = cur
        for j in range(1, win):
            s = s + a_ext[POOL_HIST - j:POOL_HIST - j + ts, cols]
        cnt = jnp.minimum(pos + 1, win).astype(F32)
        pooled = s / cnt - cur
        mixed = _dot(pooled.astype(BF16), pool_w_ref[gi])
        st_ya[slot, :, cols] = (mixed * pool_scale_ref[:, cols]).astype(BF16)
        if gi % 2 == 1:
            yield
    a_ext[0:POOL_HIST, :] = a_ext[ts:ts + POOL_HIST, :]


def _ab_stage_state(h, slot, masks, out_norm_ref, w_out_ref, o_ref, state, stage, *, ts, heads):
    hd = LANES
    n_chunks = ts // CHUNK
    n_pairs = ts // PAIR
    per_pair = PAIR // CHUNK
    eye = masks[3]
    st_l, st_rhs, st_kdec, st_dec, st_aqk, st_qd16, st_ya, st_gate = stage
    blocks = [(hh, p) for hh in range(heads) for p in range(n_pairs)]

    lmats = [st_l[slot, hh, p] for hh, p in blocks]
    l16s = [m.astype(BF16) for m in lmats]
    invs = [eye - m for m in lmats]
    pws = [_dot(m, m) for m in l16s]
    yield
    for step in range(NEUMANN_STEPS):
        p16s = [m.astype(BF16) for m in pws]
        invs = [invs[i] + _dot(invs[i].astype(BF16), p16s[i]) for i in range(len(blocks))]
        if step + 1 < NEUMANN_STEPS:
            pws = [_dot(m, m) for m in p16s]
        yield

    us, w16s, ku_kws = {}, {}, {}

    def solve_pair(p):
        for hh in range(heads):
            i = hh * n_pairs + p
            sol = _dot(invs[i].astype(BF16), st_rhs[slot, hh, pl.ds(p * PAIR, PAIR), :])
            sol16 = sol.astype(BF16)
            us[i] = sol[:, :hd]
            w16s[i] = sol16[:, hd:]
            for lc in range(per_pair):
                c = p * per_pair + lc
                k_dec = st_kdec[slot, hh, pl.ds(c * CHUNK, CHUNK), :]
                ku_kws[(hh, c)] = _dot_tn(k_dec, sol16[lc * CHUNK:(lc + 1) * CHUNK])

    solve_pair(0)
    yield

    s_cur = [state[hh] for hh in range(heads)]
    vn_parts = [[] for _ in blocks]
    o_parts = [[] for _ in range(heads)]

    def outputs_of_chunk(c, ws_qs_all):
        p, lc = divmod(c, per_pair)
        local = slice(lc * CHUNK, (lc + 1) * CHUNK)
        for hh in range(heads):
            i = hh * n_pairs + p
            ws_qs = ws_qs_all[hh]
            v_new = us[i][local] - ws_qs[:CHUNK]
            vn_parts[i].append(v_new.astype(BF16))
            pad = [jnp.zeros((CHUNK, hd), BF16)] * (per_pair - lc - 1)
            vn_buf = jnp.concatenate(vn_parts[i] + pad, axis=0)
            aqk_rows = st_aqk[slot, hh, p, pl.ds(lc * CHUNK, CHUNK), :]
            o_parts[hh].append(ws_qs[CHUNK:] + _dot(aqk_rows, vn_buf))

    pending = None
    for c in range(n_chunks):
        p, lc = divmod(c, per_pair)
        local = slice(lc * CHUNK, (lc + 1) * CHUNK)
        s16s = [s_cur[hh].astype(BF16) for hh in range(heads)]
        for hh in range(heads):
            ku_kw = ku_kws[(hh, c)]
            s_cur[hh] = (s_cur[hh] * st_dec[slot, hh, c][0:1, :] + ku_kw[:, :hd]
                         - _dot(ku_kw[:, hd:].astype(BF16), s16s[hh]))
        ws_qs_all = []
        for hh in range(heads):
            i = hh * n_pairs + p
            wq = jnp.concatenate([w16s[i][local], st_qd16[slot, hh, pl.ds(c * CHUNK, CHUNK), :]], axis=0)
            ws_qs_all.append(_dot(wq, s16s[hh]))
        if pending is not None:
            outputs_of_chunk(*pending)
        pending = (c, ws_qs_all)
        if c + 1 < n_pairs:
            solve_pair(c + 1)
        yield
    outputs_of_chunk(*pending)
    for hh in range(heads):
        state[hh] = s_cur[hh]
    gate = st_gate[slot]
    o_heads = []
    for hh in range(heads):
        o = jnp.concatenate(o_parts[hh], axis=0)
        o_heads.append((_rmsnorm(o, out_norm_ref[...]) * gate[:, hh * hd:(hh + 1) * hd]).astype(BF16))
    y = jnp.concatenate([st_ya[slot]] + o_heads, axis=1)
    o_ref[0] = h + _dot(y, w_out_ref[...])
    yield


def _mixer_ab_body(h_front_ref, h_state_ref, g_ref, w_main_ref, w_bg_ref, pool_w_ref,
                   pool_scale_ref, conv_w_ref, a_log_ref, dt_bias_ref, out_norm_ref,
                   w_out_ref, o_ref, a_ext, qkv_ext, state, *stage,
                   ts, tiles_per_row, pool_width, dn_width, heads):
    g = pl.program_id(0)
    t_front = lax.rem(g, tiles_per_row)
    t_state = lax.rem(g + tiles_per_row - 1, tiles_per_row)
    slot_front = lax.rem(g, 2)
    slot_state = 1 - slot_front

    @pl.when(g == 0)
    def _():
        for ref in stage:
            ref[...] = jnp.zeros(ref.shape, ref.dtype)

    @pl.when(t_front == 0)
    def _():
        a_ext[0:POOL_HIST, :] = jnp.zeros((POOL_HIST, pool_width), F32)
        qkv_ext[0:CONV_HIST, :] = jnp.zeros((CONV_HIST, 3 * dn_width), F32)

    @pl.when((t_state == 0) | (g == 0))
    def _():
        state[...] = jnp.zeros(state.shape, F32)

    ri = lax.broadcasted_iota(jnp.int32, (PAIR, PAIR), 0)
    ci = lax.broadcasted_iota(jnp.int32, (PAIR, PAIR), 1)
    same = (ri ^ ci) < CHUNK
    tril = same & (ci <= ri)
    strict = same & (ci < ri)
    masks = (tril, strict, jnp.where(tril, 1.0, 0.0).astype(BF16), jnp.where(ri == ci, 1.0, 0.0))

    front = _ab_stage_front(
        h_front_ref[0], t_front, slot_front, masks, g_ref, w_main_ref, w_bg_ref, pool_w_ref,
        pool_scale_ref, conv_w_ref, a_log_ref, dt_bias_ref, a_ext, qkv_ext, stage,
        ts=ts, pool_width=pool_width, dn_width=dn_width, heads=heads)
    back = _ab_stage_state(h_state_ref[0], slot_state, masks, out_norm_ref, w_out_ref, o_ref,
                           state, stage, ts=ts, heads=heads)
    for who in AB_EMISSION_ORDER:
        next(back if who == "B" else front, None)
    for gen in (back, front):
        for _ in gen:
            pass


def _mixer_ab(h, norm_g, w_main, w_bg, pool_w, pool_scale, conv_w, a_log, dt_bias, out_norm, w_out,
              layer):
    b, s, d = h.shape
    heads = a_log.shape[0]
    pool_width = pool_scale.shape[0]
    dn_width = heads * LANES
    ts = min(MIX_ROWS, s)
    tiles_per_row = s // ts
    n_tiles = b * tiles_per_row
    n_chunks = ts // CHUNK
    n_pairs = ts // PAIR
    gate_row = lambda p: jnp.pad(p, (heads, LANES - 2 * heads)).reshape(1, LANES)
    tiles = h.reshape(n_tiles, ts, d)
    front_spec = pl.BlockSpec((1, ts, d), lambda g: (jnp.minimum(g, n_tiles - 1), 0, 0))
    state_spec = pl.BlockSpec((1, ts, d), lambda g: (jnp.maximum(g - 1, 0), 0, 0))
    consts = [norm_g.reshape(1, d), w_main, w_bg, pool_w,
              pool_scale.reshape(1, pool_width), conv_w, gate_row(a_log), gate_row(dt_bias),
              out_norm.reshape(1, LANES), w_out]
    stacked = (1, 2, 3, 9)
    const_specs = [_layer_spec(a, layer) if i in stacked else _const_spec(a.shape)
                   for i, a in enumerate(consts)]
    stage_shapes = [
        pltpu.VMEM((2, heads, n_pairs, PAIR, PAIR), F32),
        pltpu.VMEM((2, heads, ts, 2 * LANES), BF16),
        pltpu.VMEM((2, heads, ts, LANES), BF16),
        pltpu.VMEM((2, heads, n_chunks, SUBLANES, LANES), F32),
        pltpu.VMEM((2, heads, n_pairs, PAIR, PAIR), BF16),
        pltpu.VMEM((2, heads, ts, LANES), BF16),
        pltpu.VMEM((2, ts, pool_width), BF16),
        pltpu.VMEM((2, ts, dn_width), F32),
    ]
    out = pl.pallas_call(
        functools.partial(_mixer_ab_body, ts=ts, tiles_per_row=tiles_per_row,
                          pool_width=pool_width, dn_width=dn_width, heads=heads),
        grid=(n_tiles + 1,),
        in_specs=[front_spec, state_spec] + const_specs,
        out_specs=state_spec,
        out_shape=jax.ShapeDtypeStruct((n_tiles, ts, d), F32),
        scratch_shapes=[pltpu.VMEM((POOL_HIST + ts, pool_width), F32),
                        pltpu.VMEM((CONV_HIST + ts, 3 * dn_width), F32),
                        pltpu.VMEM((heads, LANES, LANES), F32)] + stage_shapes,
        compiler_params=pltpu.CompilerParams(
            dimension_semantics=("arbitrary",), vmem_limit_bytes=VMEM_LIMIT),
        name="mixer_ab",
    )(tiles, tiles, *consts)
    return out.reshape(b, s, d)


def _mixer_cd_body(h_ref, g_ref, w_in_ref, ln_g_ref, ln_b_ref, sgu_w_ref, sgu_bias_ref,
                   conv_w_ref, w_out_ref, o_ref, p_ext, *, ts, sgu_width, sc_width, heads):
    t = pl.program_id(1)
    n_blocks = ts // SGU_BLOCK

    @pl.when(t == 0)
    def _():
        p_ext[0:CONV_HIST, :] = jnp.zeros((CONV_HIST, sc_width), F32)

    h = h_ref[0]
    hn = _rmsnorm(h, g_ref[...]).astype(BF16)
    proj = _dot(hn, w_in_ref[...])
    uv = _gelu_tanh(proj[:, :2 * sgu_width])
    u = uv[:, :sgu_width]
    v = uv[:, sgu_width:]
    mu = jnp.mean(v, axis=-1, keepdims=True)
    vc = v - mu
    v = vc * lax.rsqrt(jnp.mean(vc * vc, axis=-1, keepdims=True) + EPS) * ln_g_ref[...] + ln_b_ref[...]
    v16 = v.astype(BF16)

    ri = lax.broadcasted_iota(jnp.int32, (SGU_BLOCK, SGU_BLOCK), 0)
    ci = lax.broadcasted_iota(jnp.int32, (SGU_BLOCK, SGU_BLOCK), 1)
    head_cols = []
    for hh in range(heads):
        hs = slice(hh * LANES, (hh + 1) * LANES)
        ws = jnp.where(ci <= ri, sgu_w_ref[hh], 0.0).astype(BF16)
        rhs = jnp.concatenate([v16[n * SGU_BLOCK:(n + 1) * SGU_BLOCK, hs] for n in range(n_blocks)],
                              axis=1)
        m = _dot(ws, rhs)
        head_cols.append(jnp.concatenate(
            [m[:, n * LANES:(n + 1) * LANES] for n in range(n_blocks)], axis=0))
    mixed = jnp.concatenate(head_cols, axis=1)
    bias = jnp.concatenate([sgu_bias_ref[...]] * n_blocks, axis=0)
    y_c = u * (mixed + bias)

    o0 = 2 * sgu_width
    xd = proj[:, o0:o0 + sc_width]
    bg = proj[:, o0 + sc_width:o0 + 2 * sc_width]
    cg = proj[:, o0 + 2 * sc_width:o0 + 3 * sc_width]
    p_ext[CONV_HIST:CONV_HIST + ts, :] = cg * xd
    conv = conv_w_ref[SC_CONV - 1:SC_CONV, :] * p_ext[CONV_HIST:CONV_HIST + ts, :]
    for j in range(SC_CONV - 1):
        lo = CONV_HIST - (SC_CONV - 1) + j
        conv = conv + conv_w_ref[j:j + 1, :] * p_ext[lo:lo + ts, :]
    p_ext[0:CONV_HIST, :] = p_ext[ts:ts + CONV_HIST, :]
    y_d = bg * conv

    y = jnp.concatenate([y_c, y_d], axis=1).astype(BF16)
    o_ref[0] = h + _dot(y, w_out_ref[...])


def _mixer_cd(h, norm_g, w_in, ln_g, ln_b, sgu_w, sgu_bias, conv_w, w_out, layer):
    b, s, d = h.shape
    heads = sgu_w.shape[0]
    sgu_width = ln_g.shape[0]
    sc_width = conv_w.shape[1]
    ts = min(CD_ROWS, s)
    bias_x = jnp.repeat(sgu_bias.T, LANES, axis=1)
    tok_spec = pl.BlockSpec((1, ts, d), lambda i, j: (i, j, 0))
    args = [h, norm_g.reshape(1, d), w_in, ln_g.reshape(1, sgu_width),
            ln_b.reshape(1, sgu_width), sgu_w, bias_x, conv_w, w_out]
    stacked = (2, 8)
    in_specs = [tok_spec] + [_layer_spec(a, layer) if i in stacked else _const_spec(a.shape)
                             for i, a in enumerate(args) if i > 0]
    return pl.pallas_call(
        functools.partial(_mixer_cd_body, ts=ts, sgu_width=sgu_width, sc_width=sc_width,
                          heads=heads),
        grid=(b, s // ts),
        in_specs=in_specs,
        out_specs=tok_spec,
        out_shape=jax.ShapeDtypeStruct((b, s, d), F32),
        scratch_shapes=[pltpu.VMEM((CONV_HIST + ts, sc_width), F32)],
        compiler_params=pltpu.CompilerParams(
            dimension_semantics=("arbitrary", "arbitrary"), vmem_limit_bytes=VMEM_LIMIT),
        name="mixer_cd",
    )(*args)


def kernel(x, ffn1_norm, ffn1_w_gate, ffn1_w_up, ffn1_w_down, mix_norm, ffn2_norm, ffn2_w_gate, ffn2_w_up, ffn2_w_down, ab_w_in, pool_w, pool_scale, dn_conv_w, dn_a_log, dn_dt_bias, dn_out_norm, ab_w_out, cd_w_in, sgu_norm_g, sgu_norm_b, sgu_w, sgu_bias, sc_conv_w, cd_w_out, final_norm):
    b, s, d = x.shape
    depth = ffn1_norm.shape[0]
    h = x.reshape(b * s, d)
    ffn1 = [w.astype(BF16) for w in (ffn1_w_gate, ffn1_w_up, ffn1_w_down)]
    ffn2 = [w.astype(BF16) for w in (ffn2_w_gate, ffn2_w_up, ffn2_w_down)]
    ab_heads = dn_a_log.shape[1]
    ab_main = pool_scale.shape[1] + 4 * ab_heads * LANES
    ab_w_main = ab_w_in.astype(BF16)
    ab_w_bg = jnp.pad(ab_w_in[:, :, ab_main:ab_main + 2 * ab_heads],
                      ((0, 0), (0, 0), (0, LANES - 2 * ab_heads))).astype(BF16)
    ab_pool_w, ab_out = pool_w.astype(BF16), ab_w_out.astype(BF16)
    cd_in, cd_out = cd_w_in.astype(BF16), cd_w_out.astype(BF16)
    for layer in range(depth):
        h = _ffn(h, ffn1_norm[layer], *ffn1, layer)
        h3 = h.reshape(b, s, d)
        if layer % 2 == 0:
            e = layer // 2
            h3 = _mixer_ab(h3, mix_norm[layer], ab_w_main, ab_w_bg, ab_pool_w, pool_scale[e],
                           dn_conv_w[e], dn_a_log[e], dn_dt_bias[e], dn_out_norm[e], ab_out, e)
        else:
            o = layer // 2
            h3 = _mixer_cd(h3, mix_norm[layer], cd_in, sgu_norm_g[o], sgu_norm_b[o], sgu_w[o],
                           sgu_bias[o], sc_conv_w[o], cd_out, o)
        h = h3.reshape(b * s, d)
        h = _ffn(h, ffn2_norm[layer], *ffn2, layer,
                 final_g=final_norm if layer == depth - 1 else None)
    return h.reshape(b, s, d)
```
